```python
import math
import functools
import jax
import jax.numpy as jnp
from jax import lax
import numpy as np

D_MODEL = 1024
BATCH = 8
SEQ = 8192
DEPTH = 4

GRID_W = 64
CTX_LEN = 256
N_MIXERS = 4
NORM_EPS = 1e-6
ROPE_THETA = 10000.0

LRU_WIDTH = 2 * D_MODEL
LRU_BLOCKS = 16
LRU_BLOCK = LRU_WIDTH // LRU_BLOCKS
LRU_C = 8.0
LRU_CONV = 4
LRU_PAD = (2, 1)

RET_HEAD_QK = 256
RET_HEADS = D_MODEL // RET_HEAD_QK
RET_HEAD_V = 2 * RET_HEAD_QK
RET_CHUNK = 128

HY_WIDTH = D_MODEL
HY_ORDER = 2
HY_SHORT = 3
HY_EMB = 33
HY_HIDDEN = 64
HY_FAST_DECAY = 0.3
HY_SLOW_DECAY = 1.5
HY_TARGET = 1e-2

ATT_HEADS = 8
ATT_KV_HEADS = 2
ATT_HEAD_DIM = D_MODEL // ATT_HEADS
ATT_BLOCK = 128

kernel_name = 'hybrid_lru_ret_hyena_gqa_prefix_dit'

F32 = jnp.float32


def _rms(x):
    xf = x.astype(F32)
    return xf * lax.rsqrt(jnp.mean(xf * xf, axis=-1, keepdims=True) + NORM_EPS)


def rmsnorm(x, g):
    return (_rms(x) * g.astype(F32)).astype(x.dtype)


def dwconv(x, w, b, pad):
    y = lax.conv_general_dilated(x, w[:, None, :].astype(x.dtype), window_strides=(1,), padding=[pad],
                                 dimension_numbers=('NWC', 'WIO', 'NWC'), feature_group_count=x.shape[-1])
    return y + b.astype(x.dtype)


def rope_angles(pos, dim):
    half = dim // 2
    inv = ROPE_THETA ** (-jnp.arange(half, dtype=F32) / half)
    return pos[:, None] * inv[None, :]


def axial_angles(L, dim):
    rows = L // GRID_W
    row = jnp.repeat(jnp.arange(rows, dtype=F32), GRID_W)
    col = jnp.tile(jnp.arange(GRID_W, dtype=F32), rows)
    return jnp.concatenate([rope_angles(row, dim // 2), rope_angles(col, dim // 2)], axis=-1)


def apply_rotary(x, ang):
    xf = x.astype(F32)
    half = xf.shape[-1] // 2
    cos = jnp.cos(ang)[None, :, None, :]
    sin = jnp.sin(ang)[None, :, None, :]
    x1, x2 = xf[..., :half], xf[..., half:]
    return jnp.concatenate([x1 * cos - x2 * sin, x1 * sin + x2 * cos], axis=-1).astype(x.dtype)


def modulation(cond, mod_w, mod_b):
    m = jax.nn.silu(cond) @ mod_w + mod_b
    return jnp.split(m, 3, axis=-1)


def _lin_combine(e1, e2):
    a1, b1 = e1
    a2, b2 = e2
    return a1 * a2, a2 * b1 + b2


def rglru_scan(xs, w_r, b_r, w_i, b_i, lam, h0, reverse):
    bsz, L, W = xs.shape
    xb = xs.reshape(bsz, L, LRU_BLOCKS, LRU_BLOCK)
    r = jax.nn.sigmoid(jnp.einsum('blnj,njk->blnk', xb, w_r.astype(F32)).reshape(bsz, L, W) + b_r.astype(F32))
    i = jax.nn.sigmoid(jnp.einsum('blnj,njk->blnk', xb, w_i.astype(F32)).reshape(bsz, L, W) + b_i.astype(F32))
    log_a = -LRU_C * r * jax.nn.softplus(-lam.astype(F32))
    a = jnp.exp(log_a)
    b = jnp.sqrt(-jnp.expm1(2.0 * log_a)) * (i * xs)
    a_cum, b_cum = lax.associative_scan(_lin_combine, (a, b), axis=1, reverse=reverse)
    return a_cum * h0[:, None, :] + b_cum


def mixer_lru(h, hc, need_ctx_out, w_in, conv_w, conv_b, w_r, b_r, w_i, b_i, lam, w_out):
    W = LRU_WIDTH
    u = h @ w_in
    xl = dwconv(u[..., :W], conv_w, conv_b, LRU_PAD).astype(F32)
    uc = hc @ (w_in if need_ctx_out else w_in[:, :W])
    xc = dwconv(uc[..., :W], conv_w, conv_b, LRU_PAD).astype(F32)
    zero = jnp.zeros((h.shape[0], W), F32)
    lat, con = [], []
    for d, rev in enumerate((False, True)):
        hcs = rglru_scan(xc, w_r[d], b_r[d], w_i[d], b_i[d], lam[d], zero, rev)
        h0 = hcs[:, 0] if rev else hcs[:, -1]
        lat.append(rglru_scan(xl, w_r[d], b_r[d], w_i[d], b_i[d], lam[d], h0, rev))
        con.append(hcs)
    y = ((lat[0] + lat[1]).astype(h.dtype) * jax.nn.silu(u[..., W:])) @ w_out
    yc = None
    if need_ctx_out:
        yc = ((con[0] + con[1]).astype(hc.dtype) * jax.nn.silu(uc[..., W:])) @ w_out
    return y, yc


def retention_context_states(k, v, logg_f, logg_b):
    Lc = k.shape[1]
    m = jnp.arange(Lc, dtype=F32)[:, None]
    wf = jnp.exp((Lc - 1 - m) * logg_f[None, :])
    wb = jnp.exp(m * logg_b[None, :])
    sf = jnp.einsum('bmhd,bmhe,mh->bhde', k, v, wf)
    sb = jnp.einsum('bmhd,bmhe,mh->bhde', k, v, wb)
    return sf, sb


def retention_bidir(q, k, v, logg_f, logg_b, s0_f, s0_b):
    bsz, L, H, dk = q.shape
    dv = v.shape[-1]
    C = RET_CHUNK
    n = L // C
    qc = q.reshape(bsz, n, C, H, dk)
    kc = k.reshape(bsz, n, C, H, dk)
    vc = v.reshape(bsz, n, C, H, dv)
    pos = jnp.arange(C, dtype=F32)
    diff = pos[:, None] - pos[None, :]
    dist = jnp.abs(diff)[None]
    mask = jnp.where((diff >= 0)[None], jnp.exp(dist * logg_f[:, None, None]),
                     jnp.exp(dist * logg_b[:, None, None]))
    s = jnp.einsum('bnihd,bnjhd->bnhij', qc, kc) * mask
    o = jnp.einsum('bnhij,bnjhe->bnihe', s, vc)
    qf_dec = jnp.exp((pos[:, None] + 1.0) * logg_f[None, :])
    kf_dec = jnp.exp((C - 1.0 - pos[:, None]) * logg_f[None, :])
    qb_dec = jnp.exp((C - pos[:, None]) * logg_b[None, :])
    kb_dec = jnp.exp(pos[:, None] * logg_b[None, :])
    blk_f = jnp.exp(C * logg_f)[None, :, None, None]
    blk_b = jnp.exp(C * logg_b)[None, :, None, None]
    xs = (jnp.moveaxis(qc, 1, 0), jnp.moveaxis(kc, 1, 0), jnp.moveaxis(vc, 1, 0))

    def make_step(q_dec, k_dec, blk):
        def step(state, qkv):
            qq, kk, vv = qkv
            out = jnp.einsum('bihd,bhde->bihe', qq, state) * q_dec[None, :, :, None]
            state = state * blk + jnp.einsum('bjhd,bjhe,jh->bhde', kk, vv, k_dec)
            return state, out
        return step

    _, of = lax.scan(make_step(qf_dec, kf_dec, blk_f), s0_f, xs)
    _, ob = lax.scan(make_step(qb_dec, kb_dec, blk_b), s0_b, xs, reverse=True)
    o = o + jnp.moveaxis(of + ob, 0, 1)
    return o.reshape(bsz, L, H, dv)


def mixer_ret(h, hc, need_ctx_out, ang, w_in, decay_logit, w_out):
    H, dk, dv = RET_HEADS, RET_HEAD_QK, RET_HEAD_V
    nk, nv = H * dk, H * dv
    bsz, L, _ = h.shape
    Lc = hc.shape[1]
    logg = -jax.nn.softplus(-decay_logit.astype(F32))
    u = h @ w_in
    q = apply_rotary(u[..., :nk].reshape(bsz, L, H, dk), ang).astype(F32)
    k = apply_rotary(u[..., nk:2 * nk].reshape(bsz, L, H, dk), ang).astype(F32) * dk ** -0.5
    v = u[..., 2 * nk:2 * nk + nv].reshape(bsz, L, H, dv).astype(F32)
    if need_ctx_out:
        uc = hc @ w_in
        kvc = uc[..., nk:2 * nk + nv]
    else:
        kvc = hc @ w_in[:, nk:2 * nk + nv]
    kc = kvc[..., :nk].reshape(bsz, Lc, H, dk).astype(F32) * dk ** -0.5
    vc = kvc[..., nk:].reshape(bsz, Lc, H, dv).astype(F32)
    sf, sb = retention_context_states(kc, vc, logg[0], logg[1])
    o = retention_bidir(q, k, v, logg[0], logg[1], sf, sb)
    y = (jax.nn.silu(u[..., 2 * nk + nv:]) * _rms(o).reshape(bsz, L, nv).astype(h.dtype)) @ w_out
    yc = None
    if need_ctx_out:
        qc = uc[..., :nk].reshape(bsz, Lc, H, dk).astype(F32)
        zero = jnp.zeros((bsz, H, dk, dv), F32)
        oc = retention_bidir(qc, kc, vc, logg[0], logg[1], zero, zero)
        yc = (jax.nn.silu(uc[..., 2 * nk + nv:]) * _rms(oc).reshape(bsz, Lc, nv).astype(hc.dtype)) @ w_out
    return y, yc


def hyena_kernel_fft(L, fw1, fb1, fw2, fb2, fw3, freq):
    W = HY_WIDTH
    bands = (HY_EMB - 1) // 2
    t = jnp.linspace(0.0, 1.0, L, dtype=F32)[:, None]
    w = 2.0 * math.pi * jnp.arange(L, dtype=F32)[:, None] / L
    fr = jnp.linspace(1e-4, bands - 1, bands, dtype=F32)[None, :]
    z = jnp.concatenate([t, jnp.cos(fr * w), -jnp.sin(fr * w)], axis=-1)
    fq = freq.astype(F32)
    hid = jnp.sin(fq * (z @ fw1.astype(F32) + fb1.astype(F32)))
    hid = jnp.sin(fq * (hid @ fw2.astype(F32) + fb2.astype(F32)))
    filt = (hid @ fw3.astype(F32)).reshape(L, HY_ORDER, 2, W)
    deltas = jnp.abs(jnp.linspace(math.log(HY_TARGET) / HY_SLOW_DECAY, math.log(HY_TARGET) / HY_FAST_DECAY, W, dtype=F32))
    filt = filt * jnp.exp(-t * deltas[None, :])[:, None, None, :]
    kern = jnp.concatenate([filt[:, :, 0], jnp.zeros((1, HY_ORDER, W), F32), filt[:0:-1, :, 1]], axis=0)
    kern = kern / jnp.sum(jnp.abs(kern), axis=0, keepdims=True)
    return jnp.fft.rfft(kern, axis=0)


def long_conv(u, kf, bias):
    L = u.shape[1]
    y = jnp.fft.irfft(jnp.fft.rfft(u, n=2 * L, axis=1) * kf[None], n=2 * L, axis=1)[:, :L]
    return y + u * bias


def mixer_hyena(h, hc, need_ctx_out, w_in, conv_w, conv_b, fw1, fb1, fw2, fb2, fw3, freq, skip, w_out):
    W = HY_WIDTH
    pad = (HY_SHORT // 2, HY_SHORT // 2)

    def branch(s):
        u = s @ w_in
        z = dwconv(u[..., :3 * W], conv_w, conv_b, pad).astype(F32)
        v, x1, x2 = z[..., :W], z[..., W:2 * W], z[..., 2 * W:]
        kf = hyena_kernel_fft(s.shape[1], fw1, fb1, fw2, fb2, fw3, freq)
        sk = skip.astype(F32)
        y = x1 * long_conv(v, kf[:, 0], sk[0])
        y = x2 * long_conv(y, kf[:, 1], sk[1])
        return (y.astype(s.dtype) * jax.nn.silu(u[..., 3 * W:])) @ w_out

    return branch(h), (branch(hc) if need_ctx_out else None)


def attend(q, k, v):
    s = jnp.einsum('bqgrd,bkgd->bgrqk', q, k).astype(F32) * (q.shape[-1] ** -0.5)
    p = jax.nn.softmax(s, axis=-1).astype(v.dtype)
    return jnp.einsum('bgrqk,bkgd->bqgrd', p, v)


def mixer_attn(h, hc, need_ctx_out, ang, w_in, q_norm_g, k_norm_g, w_out):
    Hq, G, d = ATT_HEADS, ATT_KV_HEADS, ATT_HEAD_DIM
    R = Hq // G
    nq, nk = Hq * d, G * d
    bsz, L, _ = h.shape
    Lc = hc.shape[1]
    u = h @ w_in
    q = apply_rotary(rmsnorm(u[..., :nq].reshape(bsz, L, Hq, d), q_norm_g), ang).reshape(bsz, L, G, R, d)
    k = apply_rotary(rmsnorm(u[..., nq:nq + nk].reshape(bsz, L, G, d), k_norm_g), ang)
    v = u[..., nq + nk:nq + 2 * nk].reshape(bsz, L, G, d)
    uc = hc @ (w_in if need_ctx_out else w_in[:, nq:nq + 2 * nk])
    off = nq if need_ctx_out else 0
    kc = rmsnorm(uc[..., off:off + nk].reshape(bsz, Lc, G, d), k_norm_g)
    vc = uc[..., off + nk:off + 2 * nk].reshape(bsz, Lc, G, d)
    keys = jnp.concatenate([k, kc], axis=1)
    vals = jnp.concatenate([v, vc], axis=1)
    qb = jnp.moveaxis(q.reshape(bsz, L // ATT_BLOCK, ATT_BLOCK, G, R, d), 1, 0)
    o = lax.map(lambda blk: attend(blk, keys, vals), qb)
    o = jnp.moveaxis(o, 0, 1).reshape(bsz, L, nq)
    y = (jax.nn.silu(u[..., nq + 2 * nk:]) * o) @ w_out
    yc = None
    if need_ctx_out:
        qc = rmsnorm(uc[..., :nq].reshape(bsz, Lc, Hq, d), q_norm_g).reshape(bsz, Lc, G, R, d)
        oc = attend(qc, kc, vc).reshape(bsz, Lc, nq)
        yc = (jax.nn.silu(uc[..., nq + 2 * nk:]) * oc) @ w_out
    return y, yc


def trunk_layer(x, ctx, c, c_ctx, mod_w, mod_b, norm_g, mixer, need_ctx_out):
    sh, sc, gt = modulation(c[:, None, :], mod_w, mod_b)
    shc, scc, gtc = modulation(c_ctx, mod_w, mod_b)
    h = rmsnorm(x, norm_g) * (1.0 + sc) + sh
    hc = rmsnorm(ctx, norm_g) * (1.0 + scc) + shc
    y, yc = mixer(h, hc, need_ctx_out)
    x = x + gt * y
    if need_ctx_out:
        ctx = ctx + gtc * yc
    return x, ctx


def setup_inputs(seed: int = 0) -> dict:
    key = jax.random.key(seed)
    ks = iter(jax.random.split(key, 64))
    D = D_MODEL

    def nrm(shape, scale):
        return scale * jax.random.normal(next(ks), shape, F32)

    def gain(n):
        return 1.0 + nrm((n,), 0.05)

    inp = {}
    inp['x'] = nrm((BATCH, SEQ, D), 1.0)
    inp['c'] = nrm((BATCH, D), 1.0)
    inp['ctx'] = nrm((BATCH, CTX_LEN, D), 1.0)
    inp['c_ctx'] = nrm((D,), 1.0)

    def adaln(p):
        inp[p + '_mod_w'] = nrm((D, 3 * D), 0.5 * D ** -0.5)
        inp[p + '_mod_b'] = nrm((3 * D,), 0.01)
        inp[p + '_norm_g'] = gain(D)

    adaln('lru')
    W = LRU_WIDTH
    inp['lru_w_in'] = nrm((D, 2 * W), D ** -0.5)
    inp['lru_conv_w'] = nrm((LRU_CONV, W), LRU_CONV ** -0.5)
    inp['lru_conv_b'] = nrm((W,), 0.01)
    inp['lru_w_r'] = nrm((2, LRU_BLOCKS, LRU_BLOCK, LRU_BLOCK), LRU_BLOCK ** -0.5)
    inp['lru_b_r'] = nrm((2, W), 0.01)
    inp['lru_w_i'] = nrm((2, LRU_BLOCKS, LRU_BLOCK, LRU_BLOCK), LRU_BLOCK ** -0.5)
    inp['lru_b_i'] = nrm((2, W), 0.01)
    a_c = jax.random.uniform(next(ks), (2, W), F32, 0.9, 0.999)
    s = a_c ** (1.0 / LRU_C)
    inp['lru_lambda'] = jnp.log(s) - jnp.log1p(-s)
    inp['lru_w_out'] = nrm((W, D), W ** -0.5)

    adaln('ret')
    nk, nv = RET_HEADS * RET_HEAD_QK, RET_HEADS * RET_HEAD_V
    inp['ret_w_in'] = nrm((D, 2 * nk + 2 * nv), D ** -0.5)
    hh = jnp.arange(RET_HEADS, dtype=F32)
    gam = 1.0 - 2.0 ** (-5.0 - hh)
    inp['ret_decay_logit'] = (jnp.log(gam) - jnp.log1p(-gam))[None, :] + nrm((2, RET_HEADS), 0.01)
    inp['ret_w_out'] = nrm((nv, D), nv ** -0.5)

    adaln('hy')
    Wh = HY_WIDTH
    inp['hy_w_in'] = nrm((D, 4 * Wh), D ** -0.5)
    inp['hy_conv_w'] = nrm((HY_SHORT, 3 * Wh), HY_SHORT ** -0.5)
    inp['hy_conv_b'] = nrm((3 * Wh,), 0.01)
    inp['hy_fw1'] = nrm((HY_EMB, HY_HIDDEN), HY_EMB ** -0.5)
    inp['hy_fb1'] = nrm((HY_HIDDEN,), 0.02)
    inp['hy_fw2'] = nrm((HY_HIDDEN, HY_HIDDEN), HY_HIDDEN ** -0.5)
    inp['hy_fb2'] = nrm((HY_HIDDEN,), 0.02)
    inp['hy_fw3'] = nrm((HY_HIDDEN, HY_ORDER * 2 * Wh), HY_HIDDEN ** -0.5)
    inp['hy_freq'] = gain(HY_HIDDEN)
    inp['hy_skip'] = nrm((HY_ORDER, Wh), 0.5)
    inp['hy_w_out'] = nrm((Wh, D), Wh ** -0.5)

    adaln('att')
    nq, nkv = ATT_HEADS * ATT_HEAD_DIM, ATT_KV_HEADS * ATT_HEAD_DIM
    inp['att_w_in'] = nrm((D, 2 * nq + 2 * nkv), D ** -0.5)
    inp['att_q_norm_g'] = gain(ATT_HEAD_DIM)
    inp['att_k_norm_g'] = gain(ATT_HEAD_DIM)
    inp['att_w_out'] = nrm((nq, D), nq ** -0.5)

    inp['final_norm_g'] = gain(D)
    return inp


def reference(x, c, ctx, c_ctx,
              lru_mod_w, lru_mod_b, lru_norm_g, lru_w_in, lru_conv_w, lru_conv_b, lru_w_r, lru_b_r,
              lru_w_i, lru_b_i, lru_lambda, lru_w_out,
              ret_mod_w, ret_mod_b, ret_norm_g, ret_w_in, ret_decay_logit, ret_w_out,
              hy_mod_w, hy_mod_b, hy_norm_g, hy_w_in, hy_conv_w, hy_conv_b, hy_fw1, hy_fb1, hy_fw2, hy_fb2,
              hy_fw3, hy_freq, hy_skip, hy_w_out,
              att_mod_w, att_mod_b, att_norm_g, att_w_in, att_q_norm_g, att_k_norm_g, att_w_out,
              final_norm_g):
    L = x.shape[1]
    ret_ang = rope_angles(jnp.arange(L, dtype=F32), RET_HEAD_QK)
    att_ang = axial_angles(L, ATT_HEAD_DIM)
    layers = [
        (lru_mod_w, lru_mod_b, lru_norm_g,
         functools.partial(mixer_lru, w_in=lru_w_in, conv_w=lru_conv_w, conv_b=lru_conv_b, w_r=lru_w_r,
                           b_r=lru_b_r, w_i=lru_w_i, b_i=lru_b_i, lam=lru_lambda, w_out=lru_w_out)),
        (ret_mod_w, ret_mod_b, ret_norm_g,
         functools.partial(mixer_ret, ang=ret_ang, w_in=ret_w_in, decay_logit=ret_decay_logit, w_out=ret_w_out)),
        (hy_mod_w, hy_mod_b, hy_norm_g,
         functools.partial(mixer_hyena, w_in=hy_w_in, conv_w=hy_conv_w, conv_b=hy_conv_b, fw1=hy_fw1, fb1=hy_fb1,
                           fw2=hy_fw2, fb2=hy_fb2, fw3=hy_fw3, freq=hy_freq, skip=hy_skip, w_out=hy_w_out)),
        (att_mod_w, att_mod_b, att_norm_g,
         functools.partial(mixer_attn, ang=att_ang, w_in=att_w_in, q_norm_g=att_q_norm_g,
                           k_norm_g=att_k_norm_g, w_out=att_w_out)),
    ]
    for i in range(DEPTH):
        mod_w, mod_b, norm_g, mixer = layers[i]
        x, ctx = trunk_layer(x, ctx, c, c_ctx, mod_w, mod_b, norm_g, mixer, i < DEPTH - 1)
    return rmsnorm(x, final_norm_g)
```

```python
import functools
import math

import numpy as np
import jax
import jax.numpy as jnp
from jax import lax
from jax.experimental import pallas as pl
from jax.experimental.pallas import tpu as pltpu

F32 = jnp.float32
BF16 = jnp.bfloat16
HIGHEST = lax.Precision.HIGHEST

NORM_EPS = 1e-6
ROPE_THETA = 10000.0
GRID_W = 64
LANES = 128
SUBLANES = 8
VMEM_LIMIT = 56 * 1024 * 1024

TOK_TILE = 256

LRU_C = 8.0
LRU_BLOCK = 128
LRU_T = 64
LRU_WC = 512

RET_HEADS = 4
RET_DK = 256
RET_DV = 512
RET_CHUNK = 256

HY_EMB = 33
HY_HIDDEN = 64
HY_FAST_DECAY = 0.3
HY_SLOW_DECAY = 1.5
HY_TARGET = 1e-2
HY_CB = 8

ATT_HEADS = 8
ATT_KV = 2
ATT_D = 128
ATT_TQ = 256
ATT_TK_CANDIDATES = (768, 512, 384, 256, 128)


def _cparams(sem):
    return pltpu.CompilerParams(dimension_semantics=sem, vmem_limit_bytes=VMEM_LIMIT)


def _dot(a, b, precision=None):
    return lax.dot_general(a, b, (((1,), (0,)), ((), ())), precision=precision,
                           preferred_element_type=F32)


def _dot_nt(a, b, precision=None):
    return lax.dot_general(a, b, (((1,), (1,)), ((), ())), precision=precision,
                           preferred_element_type=F32)


def _dot_tn(a, b):
    return lax.dot_general(a, b, (((0,), (0,)), ((), ())), preferred_element_type=F32)


def _sigmoid(x):
    return 1.0 / (1.0 + jnp.exp(-x))


def _silu(x):
    return x * _sigmoid(x)


def _mod_kernel(c_ref, w_ref, b_ref, o_ref):
    o_ref[...] = _dot(_silu(c_ref[...]), w_ref[...], HIGHEST) + b_ref[...]


def _modulation(cvec, mod_w, mod_b):
    D = cvec.shape[1]
    return pl.pallas_call(
        _mod_kernel,
        grid=(3,),
        in_specs=[pl.BlockSpec((16, D), lambda n: (0, 0)),
                  pl.BlockSpec((D, D), lambda n: (0, n)),
                  pl.BlockSpec((1, D), lambda n: (0, n))],
        out_specs=pl.BlockSpec((16, D), lambda n: (0, n)),
        out_shape=jax.ShapeDtypeStruct((16, 3 * D), F32),
        compiler_params=_cparams(("arbitrary",)),
        name="modulation",
    )(cvec, mod_w, mod_b.reshape(1, 3 * D))


def _nmm_kernel(mode, x_ref, mod_ref, g_ref, w_ref, *rest):
    o_ref = rest[-1]
    x = x_ref[0]
    ms = jnp.mean(x * x, axis=-1, keepdims=True)
    xn = x * lax.rsqrt(ms + NORM_EPS) * g_ref[...]
    mod = mod_ref[0, 0]
    h = xn * (1.0 + mod[1:2]) + mod[0:1]
    u = _dot(h.astype(BF16), w_ref[...])
    if mode == "plain":
        o_ref[0] = u.astype(o_ref.dtype)
    elif mode == "ret":
        cos = rest[0][...]
        sin = rest[1][...]
        nk = RET_HEADS * RET_DK
        half = RET_DK // 2
        for base, scale in ((0, 1.0), (nk, RET_DK ** -0.5)):
            for hh in range(RET_HEADS):
                c0 = base + hh * RET_DK
                x1 = u[:, c0:c0 + half]
                x2 = u[:, c0 + half:c0 + RET_DK]
                o_ref[0, :, c0:c0 + half] = ((x1 * cos - x2 * sin) * scale).astype(o_ref.dtype)
                o_ref[0, :, c0 + half:c0 + RET_DK] = ((x1 * sin + x2 * cos) * scale).astype(o_ref.dtype)
        o_ref[0, :, 2 * nk:] = u[:, 2 * nk:].astype(o_ref.dtype)
    elif mode == "att":
        c2 = rest[0][...]
        s2 = rest[1][...]
        qg = rest[2][...]
        kg = rest[3][...]
        nqk = ATT_HEADS + ATT_KV
        for hh in range(nqk):
            xh = u[:, hh * ATT_D:(hh + 1) * ATT_D]
            g = qg if hh < ATT_HEADS else kg
            r = lax.rsqrt(jnp.mean(xh * xh, axis=-1, keepdims=True) + NORM_EPS)
            xh = xh * r * g
            rot = xh * c2 + pltpu.roll(xh, ATT_D // 2, 1) * s2
            if hh < ATT_HEADS:
                rot = rot * (ATT_D ** -0.5)
            o_ref[0, :, hh * ATT_D:(hh + 1) * ATT_D] = rot.astype(o_ref.dtype)
        o_ref[0, :, nqk * ATT_D:] = u[:, nqk * ATT_D:].astype(o_ref.dtype)
    else:
        raise ValueError(mode)


def _norm_proj(x, modsel, norm_g, w_bf16, n_ctx_blk, mode="plain", extras=()):
    B, LT, D = x.shape
    F = w_bf16.shape[1]
    T = TOK_TILE
    tab_specs = []
    for e in extras:
        if e.shape[0] == 1:
            tab_specs.append(pl.BlockSpec(e.shape, lambda b, i: (0, 0)))
        else:
            tab_specs.append(pl.BlockSpec((T, e.shape[1]), lambda b, i: (i, 0)))
    return pl.pallas_call(
        functools.partial(_nmm_kernel, mode),
        grid=(B, LT // T),
        in_specs=[pl.BlockSpec((1, T, D), lambda b, i: (b, i, 0)),
                  pl.BlockSpec((1, 1, 3, D), lambda b, i: (jnp.where(i < n_ctx_blk, 0, 1), b, 0, 0)),
                  pl.BlockSpec((1, D), lambda b, i: (0, 0)),
                  pl.BlockSpec((D, F), lambda b, i: (0, 0), pipeline_mode=pl.Buffered(1)),
                  ] + tab_specs,
        out_specs=pl.BlockSpec((1, T, F), lambda b, i: (b, i, 0)),
        out_shape=jax.ShapeDtypeStruct((B, LT, F), BF16),
        compiler_params=_cparams(("arbitrary", "arbitrary")),
        name="norm_proj_" + mode,
    )(x, modsel, norm_g.reshape(1, D), w_bf16, *extras)


def _opr_kernel(z_ref, w_ref, x_ref, mod_ref, o_ref):
    y = _dot(z_ref[0], w_ref[...])
    o_ref[0] = x_ref[0] + mod_ref[0, 0][2:3] * y


def _out_proj_residual(z, w_bf16, x, modsel, n_ctx_blk, x_off_blk=0):
    B, Lz, Fz = z.shape
    D = x.shape[2]
    T = TOK_TILE
    return pl.pallas_call(
        _opr_kernel,
        grid=(B, Lz // T),
        in_specs=[pl.BlockSpec((1, T, Fz), lambda b, i: (b, i, 0)),
                  pl.BlockSpec((Fz, D), lambda b, i: (0, 0), pipeline_mode=pl.Buffered(1)),
                  pl.BlockSpec((1, T, D), lambda b, i: (b, i + x_off_blk, 0)),
                  pl.BlockSpec((1, 1, 3, D),
                               lambda b, i: (jnp.where(i + x_off_blk < n_ctx_blk, 0, 1), b, 0, 0))],
        out_specs=pl.BlockSpec((1, T, D), lambda b, i: (b, i, 0)),
        out_shape=jax.ShapeDtypeStruct((B, Lz, D), F32),
        compiler_params=_cparams(("arbitrary", "arbitrary")),
        name="out_proj_residual",
    )(z, w_bf16, x, modsel)


def _final_norm_kernel(x_ref, g_ref, o_ref):
    x = x_ref[0]
    ms = jnp.mean(x * x, axis=-1, keepdims=True)
    o_ref[0] = x * lax.rsqrt(ms + NORM_EPS) * g_ref[...]


def _final_norm(x, g):
    B, L, D = x.shape
    T = TOK_TILE
    return pl.pallas_call(
        _final_norm_kernel,
        grid=(B, L // T),
        in_specs=[pl.BlockSpec((1, T, D), lambda b, i: (b, i, 0)),
                  pl.BlockSpec((1, D), lambda b, i: (0, 0))],
        out_specs=pl.BlockSpec((1, T, D), lambda b, i: (b, i, 0)),
        out_shape=jax.ShapeDtypeStruct((B, L, D), F32),
        compiler_params=_cparams(("arbitrary", "arbitrary")),
        name="final_norm",
    )(x, g.reshape(1, D))


def _lru_kernel(fwd, T, n_ctx, n_tot, u_ref, up_ref, un_ref, cw_ref, cb_ref, wg_ref, bg_ref,
                lam_ref, *rest):
    if fwd:
        gate_ref, latb_ref, o_ref, h_sc, ext_sc, hs_sc = rest
    else:
        o_ref, h_sc, ext_sc = rest
    Wc = u_ref.shape[2]
    s = pl.program_id(1)
    if fwd:
        c = s
    else:
        c = jnp.where(s < n_ctx, n_ctx - 1 - s, n_tot + n_ctx - 1 - s)

    @pl.when(s == 0)
    def _():
        h_sc[...] = jnp.zeros_like(h_sc)

    first = jnp.logical_or(c == 0, c == n_ctx)
    last = jnp.logical_or(c == n_ctx - 1, c == n_tot - 1)
    ext_sc[0:2] = up_ref[...] * jnp.where(first, 0.0, 1.0)
    ext_sc[2:T + 2] = u_ref[...]
    ext_sc[T + 2:T + 3] = un_ref[...] * jnp.where(last, 0.0, 1.0)
    xl = cb_ref[...] + cw_ref[0:1, :] * ext_sc[0:T]
    for k in range(1, 4):
        xl = xl + cw_ref[k:k + 1, :] * ext_sc[k:k + T]
    xb = xl.reshape(T * SUBLANES, Wc)
    lam = lam_ref[...]
    sp = jnp.maximum(-lam, 0.0) + jnp.log1p(jnp.exp(-jnp.abs(lam)))
    for n in range(Wc // LRU_BLOCK):
        sl = slice(n * LRU_BLOCK, (n + 1) * LRU_BLOCK)
        xn = xb[:, sl]
        g = _dot(xn.astype(BF16), wg_ref[n])
        r = _sigmoid(g[:, :LRU_BLOCK] + bg_ref[0:1, sl])
        i = _sigmoid(g[:, LRU_BLOCK:] + bg_ref[1:2, sl])
        a = jnp.exp(-LRU_C * r * sp[:, sl])
        bb = jnp.sqrt(1.0 - a * a) * (i * xn)
        a3 = a.reshape(T, SUBLANES, LRU_BLOCK)
        b3 = bb.reshape(T, SUBLANES, LRU_BLOCK)
        h = h_sc[:, sl]
        for t in (range(T) if fwd else range(T - 1, -1, -1)):
            h = a3[t] * h + b3[t]
            if fwd:
                hs_sc[t, :, sl] = h
            else:
                o_ref[t, :, sl] = h
        h_sc[:, sl] = h
    if fwd:
        o_ref[...] = (hs_sc[...] + latb_ref[...]) * _silu(gate_ref[...])


def _lru_pass(fwd, ut, conv_w, conv_b, wg, bg, lam, n_ctx_rows, latb=None):
    LT, B, W2 = ut.shape
    W = W2 // 2
    T, Wc = LRU_T, LRU_WC
    n_tot = LT // T
    n_ctx = n_ctx_rows // T
    nj = W // Wc

    def chunk(s):
        if fwd:
            return s
        return jnp.where(s < n_ctx, n_ctx - 1 - s, n_tot + n_ctx - 1 - s)

    in_specs = [
        pl.BlockSpec((T, B, Wc), lambda j, s: (chunk(s), 0, j)),
        pl.BlockSpec((2, B, Wc), lambda j, s: (jnp.maximum(chunk(s) * (T // 2) - 1, 0), 0, j)),
        pl.BlockSpec((1, B, Wc), lambda j, s: (jnp.minimum((chunk(s) + 1) * T, LT - 1), 0, j)),
        pl.BlockSpec((4, Wc), lambda j, s: (0, j)),
        pl.BlockSpec((1, Wc), lambda j, s: (0, j)),
        pl.BlockSpec((Wc // LRU_BLOCK, LRU_BLOCK, 2 * LRU_BLOCK), lambda j, s: (j, 0, 0)),
        pl.BlockSpec((2, Wc), lambda j, s: (0, j)),
        pl.BlockSpec((1, Wc), lambda j, s: (0, j)),
    ]
    args = [ut, ut, ut, conv_w, conv_b.reshape(1, W), wg, bg, lam.reshape(1, W)]
    scratch = [pltpu.VMEM((B, Wc), F32), pltpu.VMEM((T + 3, B, Wc), F32)]
    if fwd:
        in_specs += [pl.BlockSpec((T, B, Wc), lambda j, s: (s, 0, nj + j)),
                     pl.BlockSpec((T, B, Wc), lambda j, s: (s, 0, j))]
        args += [ut, latb]
        scratch.append(pltpu.VMEM((T, B, Wc), F32))
    return pl.pallas_call(
        functools.partial(_lru_kernel, fwd, T, n_ctx, n_tot),
        grid=(nj, n_tot),
        in_specs=in_specs,
        out_specs=pl.BlockSpec((T, B, Wc), lambda j, s: (chunk(s), 0, j)),
        out_shape=jax.ShapeDtypeStruct((LT, B, W), F32),
        scratch_shapes=scratch,
        compiler_params=_cparams(("arbitrary", "arbitrary")),
        name="lru_fwd" if fwd else "lru_bwd",
    )(*args)


def _ret_kernel(fwd, C, lg_ref, blk_ref, q_ref, k_ref, v_ref, *rest):
    if fwd:
        gate_ref, ob_ref, o_ref, st_sc = rest
    else:
        o_ref, st_sc = rest
    s = pl.program_id(1)

    @pl.when(s == 0)
    def _():
        st_sc[...] = jnp.zeros_like(st_sc)

    d = 0 if fwd else 1
    row_v = lax.broadcasted_iota(jnp.int32, (C, RET_DV), 0).astype(F32)
    row_k = lax.broadcasted_iota(jnp.int32, (C, RET_DK), 0).astype(F32)
    if fwd:
        ri = lax.broadcasted_iota(jnp.int32, (C, C), 0)
        ci = lax.broadcasted_iota(jnp.int32, (C, C), 1)
        diff = (ri - ci).astype(F32)
    for hh in range(RET_HEADS):
        lg = lg_ref[d, hh]
        qh = q_ref[0, :, hh * RET_DK:(hh + 1) * RET_DK]
        kh = k_ref[0, :, hh * RET_DK:(hh + 1) * RET_DK]
        vh = v_ref[0, :, hh * RET_DV:(hh + 1) * RET_DV]
        if fwd:
            q_dec = jnp.exp((row_v + 1.0) * lg)
            k_dec = jnp.exp((C - 1.0 - row_k) * lg)
        else:
            q_dec = jnp.exp((C - row_v) * lg)
            k_dec = jnp.exp(row_k * lg)
        st = st_sc[hh]
        o = _dot(qh, st.astype(BF16)) * q_dec
        kd = (kh.astype(F32) * k_dec).astype(BF16)
        st_sc[hh] = st * blk_ref[d, hh] + _dot_tn(kd, vh)
        if fwd:
            mask = jnp.exp(jnp.where(diff >= 0, diff * lg_ref[0, hh], -diff * lg_ref[1, hh]))
            sc = _dot_nt(qh, kh) * mask
            o = o + _dot(sc.astype(BF16), vh)
            o = o + ob_ref[0, :, hh * RET_DV:(hh + 1) * RET_DV].astype(F32)
            o = o * lax.rsqrt(jnp.mean(o * o, axis=-1, keepdims=True) + NORM_EPS)
            gate = gate_ref[0, :, hh * RET_DV:(hh + 1) * RET_DV].astype(F32)
            o_ref[0, :, hh * RET_DV:(hh + 1) * RET_DV] = (_silu(gate) * o).astype(o_ref.dtype)
        else:
            o_ref[0, :, hh * RET_DV:(hh + 1) * RET_DV] = o.astype(o_ref.dtype)


def _ret_pass(fwd, u, logg, blk, n_ctx_rows, ob=None):
    B, LT, _ = u.shape
    C = RET_CHUNK
    nk = RET_HEADS * RET_DK
    nv = RET_HEADS * RET_DV
    n_tot = LT // C
    n_ctx = n_ctx_rows // C

    def chunk(s):
        if fwd:
            return s
        return jnp.where(s < n_ctx, n_ctx - 1 - s, n_tot + n_ctx - 1 - s)

    smem = pl.BlockSpec(memory_space=pltpu.SMEM)
    in_specs = [smem, smem,
                pl.BlockSpec((1, C, nk), lambda b, s: (b, chunk(s), 0)),
                pl.BlockSpec((1, C, nk), lambda b, s: (b, chunk(s), 1)),
                pl.BlockSpec((1, C, nv), lambda b, s: (b, chunk(s), 1))]
    args = [logg, blk, u, u, u]
    if fwd:
        in_specs += [pl.BlockSpec((1, C, nv), lambda b, s: (b, s, 2)),
                     pl.BlockSpec((1, C, nv), lambda b, s: (b, s, 0))]
        args += [u, ob]
    return pl.pallas_call(
        functools.partial(_ret_kernel, fwd, C),
        grid=(B, n_tot),
        in_specs=in_specs,
        out_specs=pl.BlockSpec((1, C, nv), lambda b, s: (b, chunk(s), 0)),
        out_shape=jax.ShapeDtypeStruct((B, LT, nv), BF16),
        scratch_shapes=[pltpu.VMEM((RET_HEADS, RET_DK, RET_DV), F32)],
        compiler_params=_cparams(("arbitrary", "arbitrary")),
        name="ret_fwd" if fwd else "ret_bwd",
    )(*args)


def _hy_filter_kernel(z_ref, tt_ref, fw1_ref, fb1_ref, fw2_ref, fb2_ref, fq_ref, w3_ref,
                      delta_ref, o_ref):
    fq = fq_ref[...]
    h1 = jnp.sin(fq * (_dot(z_ref[...], fw1_ref[...], HIGHEST) + fb1_ref[...]))
    h2 = jnp.sin(fq * (_dot(h1, fw2_ref[...], HIGHEST) + fb2_ref[...]))
    kt = _dot_nt(w3_ref[0], h2, HIGHEST)
    t = tt_ref[0, 0:1, :]
    msk = tt_ref[0, 1:2, :]
    o_ref[0] = kt * jnp.exp(-delta_ref[...] * t) * msk


def _hy_positions(Lv, R):
    N = R * LANES
    p = np.arange(N)
    fwd = p < Lv
    bwd = p > N - Lv
    tap = np.where(fwd, p, np.where(bwd, N - p, 0))
    return tap.astype(np.int32), (fwd | bwd).astype(np.float32)


def _hy_filter_taps(Lv, R, fw1, fb1, fw2, fb2, fw3, freq):
    W = fw3.shape[1] // 4
    bands = (HY_EMB - 1) // 2
    t = jnp.linspace(0.0, 1.0, Lv, dtype=F32)[:, None]
    w = 2.0 * math.pi * jnp.arange(Lv, dtype=F32)[:, None] / Lv
    fr = jnp.linspace(1e-4, bands - 1, bands, dtype=F32)[None, :]
    z = jnp.concatenate([t, jnp.cos(fr * w), -jnp.sin(fr * w)], axis=-1)
    tap, valid = _hy_positions(Lv, R)
    zp = jnp.pad(z[tap], ((0, 0), (0, LANES - HY_EMB)))
    tt = jnp.stack([t[tap, 0].reshape(R, LANES), jnp.asarray(valid).reshape(R, LANES)], axis=1)
    fw1p = jnp.pad(fw1.astype(F32), ((0, LANES - HY_EMB), (0, 0)))
    w3 = fw3.astype(F32).reshape(HY_HIDDEN, 2, 2, W)
    w3d = jnp.stack([w3[:, :, 0, :].reshape(HY_HIDDEN, 2 * W).T,
                     w3[:, :, 1, :].reshape(HY_HIDDEN, 2 * W).T], axis=0)
    deltas = jnp.abs(jnp.linspace(math.log(HY_TARGET) / HY_SLOW_DECAY,
                                  math.log(HY_TARGET) / HY_FAST_DECAY, W, dtype=F32))
    delta2 = jnp.concatenate([deltas, deltas]).reshape(2 * W, 1)
    F2 = 2 * W
    half = R // 2
    kt = pl.pallas_call(
        _hy_filter_kernel,
        grid=(R,),
        in_specs=[pl.BlockSpec((LANES, LANES), lambda r: (r, 0)),
                  pl.BlockSpec((1, 2, LANES), lambda r: (r, 0, 0)),
                  pl.BlockSpec((LANES, HY_HIDDEN), lambda r: (0, 0)),
                  pl.BlockSpec((1, HY_HIDDEN), lambda r: (0, 0)),
                  pl.BlockSpec((HY_HIDDEN, HY_HIDDEN), lambda r: (0, 0)),
                  pl.BlockSpec((1, HY_HIDDEN), lambda r: (0, 0)),
                  pl.BlockSpec((1, HY_HIDDEN), lambda r: (0, 0)),
                  pl.BlockSpec((1, F2, HY_HIDDEN), lambda r: (jnp.where(r < half, 0, 1), 0, 0)),
                  pl.BlockSpec((F2, 1), lambda r: (0, 0))],
        out_specs=pl.BlockSpec((1, F2, LANES), lambda r: (r, 0, 0)),
        out_shape=jax.ShapeDtypeStruct((R, F2, LANES), F32),
        compiler_params=_cparams(("arbitrary",)),
        name="hyena_filter_taps",
    )(zp, tt, fw1p, fb1.astype(F32).reshape(1, -1), fw2.astype(F32), fb2.astype(F32).reshape(1, -1),
      freq.astype(F32).reshape(1, -1), w3d, delta2)
    return jnp.transpose(kt, (1, 0, 2))


def _dft_constants(R0, R):
    N = R * LANES
    k1 = np.arange(R)[:, None]
    n1 = np.arange(R0)[None, :]
    f1 = np.exp(-2j * np.pi * k1 * n1 / R)
    ma = np.block([[f1.real, -f1.imag], [f1.imag, f1.real]])
    g1 = np.exp(2j * np.pi * n1.T * k1.T / R) / N
    mai = np.block([[g1.real, -g1.imag], [g1.imag, g1.real]])
    n2 = np.arange(LANES)
    tw = np.exp(-2j * np.pi * np.arange(R)[:, None] * n2[None, :] / N)
    f2 = np.exp(-2j * np.pi * n2[:, None] * n2[None, :] / LANES)
    mc = np.block([[f2.real, f2.imag], [-f2.imag, f2.real]])
    mci = np.block([[f2.real, -f2.imag], [f2.imag, f2.real]])
    f1_full = np.exp(-2j * np.pi * k1 * np.arange(R)[None, :] / R)
    as_bf = lambda m: jnp.asarray(m, F32).astype(BF16)
    return dict(ma=as_bf(ma), mai=as_bf(mai), mc=as_bf(mc), mci=as_bf(mci),
                tr=jnp.asarray(tw.real, F32), ti=jnp.asarray(tw.imag, F32),
                ma_real=as_bf(np.concatenate([f1_full.real, f1_full.imag], axis=0)))


def _hy_fft_kernel(R, cb, k_ref, ma_ref, tr_ref, ti_ref, mc_ref, or_ref, oi_ref):
    cols = [k_ref[c] for c in range(cb)]
    inv = []
    for col in cols:
        sabs = jnp.sum(jnp.sum(jnp.abs(col), axis=1, keepdims=True), axis=0, keepdims=True)
        inv.append(1.0 / sabs)
    rhs = jnp.concatenate(cols, axis=1)
    hi = rhs.astype(BF16)
    lo = (rhs - hi.astype(F32)).astype(BF16)
    a = _dot(ma_ref[...], hi) + _dot(ma_ref[...], lo)
    tr = tr_ref[...]
    ti = ti_ref[...]
    rows = []
    for c in range(cb):
        ar = a[:R, c * LANES:(c + 1) * LANES]
        ai = a[R:, c * LANES:(c + 1) * LANES]
        rows.append(jnp.concatenate([ar * tr - ai * ti, ar * ti + ai * tr], axis=1))
    lhs = jnp.concatenate(rows, axis=0)
    hi = lhs.astype(BF16)
    lo = (lhs - hi.astype(F32)).astype(BF16)
    x = _dot(hi, mc_ref[...]) + _dot(lo, mc_ref[...])
    for c in range(cb):
        or_ref[c] = x[c * R:(c + 1) * R, :LANES] * inv[c]
        oi_ref[c] = x[c * R:(c + 1) * R, LANES:] * inv[c]


def _hy_filter_spectrum(kt, R, consts):
    F2 = kt.shape[0]
    cb = HY_CB
    spec = pl.BlockSpec((cb, R, LANES), lambda j: (j, 0, 0))
    full = lambda a: pl.BlockSpec(a.shape, lambda j: (0,) * a.ndim)
    return pl.pallas_call(
        functools.partial(_hy_fft_kernel, R, cb),
        grid=(F2 // cb,),
        in_specs=[spec, full(consts["ma_real"]), full(consts["tr"]), full(consts["ti"]),
                  full(consts["mc"])],
        out_specs=[spec, spec],
        out_shape=[jax.ShapeDtypeStruct((F2, R, LANES), F32)] * 2,
        compiler_params=_cparams(("arbitrary",)),
        name="hyena_filter_fft",
    )(kt, consts["ma_real"], consts["tr"], consts["ti"], consts["mc"])


def _hy_conv_kernel(R0, R, Lv, cb, W, cw_ref, cbias_ref, skip_ref, v_ref, x1_ref, x2_ref, g_ref,
                    k0r_ref, k0i_ref, k1r_ref, k1i_ref, ma_ref, mai_ref, mc_ref, mci_ref,
                    tr_ref, ti_ref, o_ref):
    j = pl.program_id(0)
    row = lax.broadcasted_iota(jnp.int32, (R0, LANES), 0)
    lane = lax.broadcasted_iota(jnp.int32, (R0, LANES), 1)
    pos = row * LANES + lane
    tr = tr_ref[...]
    ti = ti_ref[...]

    def short_conv(ref, part, c, grp):
        x = ref[0, part, c].astype(F32)
        ch = grp * W + j * cb + c
        r = pltpu.roll(x, 1, 1)
        prev = jnp.where(lane == 0, pltpu.roll(r, 1, 0), r)
        prev = jnp.where(pos == 0, 0.0, prev)
        r = pltpu.roll(x, LANES - 1, 1)
        nxt = jnp.where(lane == LANES - 1, pltpu.roll(r, R0 - 1, 0), r)
        nxt = jnp.where(pos == R0 * LANES - 1, 0.0, nxt)
        y = cw_ref[0, ch] * prev + cw_ref[1, ch] * x + cw_ref[2, ch] * nxt + cbias_ref[ch]
        if Lv < R0 * LANES:
            y = jnp.where(pos < Lv, y, 0.0)
        return y

    def long_conv(re, im, kr_ref, ki_ref):
        rhs = jnp.concatenate([jnp.concatenate([re[c], im[c]], axis=0) for c in range(cb)], axis=1)
        a = _dot(ma_ref[...], rhs.astype(BF16))
        rows = []
        for c in range(cb):
            ar = a[:R, c * LANES:(c + 1) * LANES]
            ai = a[R:, c * LANES:(c + 1) * LANES]
            rows.append(jnp.concatenate([ar * tr - ai * ti, ar * ti + ai * tr], axis=1))
        x = _dot(jnp.concatenate(rows, axis=0).astype(BF16), mc_ref[...])
        rows = []
        for c in range(cb):
            xr = x[c * R:(c + 1) * R, :LANES]
            xi = x[c * R:(c + 1) * R, LANES:]
            kr = kr_ref[c]
            ki = ki_ref[c]
            rows.append(jnp.concatenate([xr * kr - xi * ki, xr * ki + xi * kr], axis=1))
        b = _dot(jnp.concatenate(rows, axis=0).astype(BF16), mci_ref[...])
        cols = []
        for c in range(cb):
            br = b[c * R:(c + 1) * R, :LANES]
            bi = b[c * R:(c + 1) * R, LANES:]
            cols.append(jnp.concatenate([br * tr + bi * ti, bi * tr - br * ti], axis=0))
        y = _dot(mai_ref[...], jnp.concatenate(cols, axis=1).astype(BF16))
        out_re = [y[:R0, c * LANES:(c + 1) * LANES] for c in range(cb)]
        out_im = [y[R0:, c * LANES:(c + 1) * LANES] for c in range(cb)]
        return out_re, out_im

    v = [[short_conv(v_ref, p, c, 0) for c in range(cb)] for p in range(2)]
    x1 = [[short_conv(x1_ref, p, c, 1) for c in range(cb)] for p in range(2)]
    cr, ci = long_conv(v[0], v[1], k0r_ref, k0i_ref)
    conv = (cr, ci)
    y = [[x1[p][c] * (conv[p][c] + v[p][c] * skip_ref[0, j * cb + c]) for c in range(cb)]
         for p in range(2)]
    x2 = [[short_conv(x2_ref, p, c, 2) for c in range(cb)] for p in range(2)]
    cr, ci = long_conv(y[0], y[1], k1r_ref, k1i_ref)
    conv = (cr, ci)
    for p in range(2):
        for c in range(cb):
            y2 = x2[p][c] * (conv[p][c] + y[p][c] * skip_ref[1, j * cb + c])
            gate = g_ref[0, p, c].astype(F32)
            o_ref[0, p, c] = (y2 * _silu(gate)).astype(o_ref.dtype)


def _hy_conv(ut, kfr, kfi, conv_w, conv_b, skip, Lv, R0, R, consts):
    G, _, W4, _, _ = ut.shape
    W = W4 // 4
    cb = HY_CB
    nj = W // cb
    smem = pl.BlockSpec(memory_space=pltpu.SMEM)
    full = lambda a: pl.BlockSpec(a.shape, lambda j, g: (0,) * a.ndim)

    def data(grp):
        return pl.BlockSpec((1, 2, cb, R0, LANES), lambda j, g: (g, 0, grp * nj + j, 0, 0))

    def filt(order):
        return pl.BlockSpec((cb, R, LANES), lambda j, g: (order * nj + j, 0, 0))

    return pl.pallas_call(
        functools.partial(_hy_conv_kernel, R0, R, Lv, cb, W),
        grid=(nj, G),
        in_specs=[smem, smem, smem, data(0), data(1), data(2), data(3),
                  filt(0), filt(0), filt(1), filt(1),
                  full(consts["ma"]), full(consts["mai"]), full(consts["mc"]), full(consts["mci"]),
                  full(consts["tr"]), full(consts["ti"])],
        out_specs=pl.BlockSpec((1, 2, cb, R0, LANES), lambda j, g: (g, 0, j, 0, 0)),
        out_shape=jax.ShapeDtypeStruct((G, 2, W, R0, LANES), BF16),
        compiler_params=_cparams(("arbitrary", "arbitrary")),
        name="hyena_conv",
    )(conv_w, conv_b, skip, ut, ut, ut, ut, kfr, kfi, kfr, kfi,
      consts["ma"], consts["mai"], consts["mc"], consts["mci"], consts["tr"], consts["ti"])


def _hy_segment(u_seg, Lv, conv_w, conv_b, fw1, fb1, fw2, fb2, fw3, freq, skip):
    B, _, W4 = u_seg.shape
    W = W4 // 4
    rows = -(-Lv // LANES)
    R0 = -(-rows // SUBLANES) * SUBLANES
    R = max(R0, -(-(2 * rows) // SUBLANES) * SUBLANES)
    consts = _dft_constants(R0, R)
    kt = _hy_filter_taps(Lv, R, fw1, fb1, fw2, fb2, fw3, freq)
    kfr, kfi = _hy_filter_spectrum(kt, R, consts)
    ut = jnp.transpose(u_seg, (0, 2, 1))
    ut = jnp.pad(ut, ((0, 0), (0, 0), (0, R0 * LANES - Lv)))
    ut = ut.reshape(B // 2, 2, W4, R0, LANES)
    z = _hy_conv(ut, kfr, kfi, conv_w.astype(F32), conv_b.astype(F32), skip.astype(F32),
                 Lv, R0, R, consts)
    z = z.reshape(B, W, R0 * LANES)[:, :, :Lv]
    return jnp.transpose(z, (0, 2, 1))


def _attn_kernel(q_ref, k_ref, v_ref, g_ref, o_ref, m_sc, l_sc, acc_sc):
    R = ATT_HEADS // ATT_KV
    j = pl.program_id(3)

    @pl.when(j == 0)
    def _():
        m_sc[...] = jnp.full_like(m_sc, -jnp.inf)
        l_sc[...] = jnp.zeros_like(l_sc)
        acc_sc[...] = jnp.zeros_like(acc_sc)

    q = jnp.concatenate([q_ref[0, :, r * ATT_D:(r + 1) * ATT_D] for r in range(R)], axis=0)
    s = _dot_nt(q, k_ref[0])
    m_prev = m_sc[...]
    m_new = jnp.maximum(m_prev, jnp.max(s, axis=-1, keepdims=True))
    alpha = jnp.exp(m_prev - m_new)
    p = jnp.exp(s - m_new)
    l_sc[...] = alpha * l_sc[...] + jnp.sum(p, axis=-1, keepdims=True)
    acc_sc[...] = alpha * acc_sc[...] + _dot(p.astype(BF16), v_ref[0])
    m_sc[...] = m_new

    @pl.when(j == pl.num_programs(3) - 1)
    def _():
        tq = q_ref.shape[1]
        o = acc_sc[...] / l_sc[...]
        for r in range(R):
            gate = g_ref[0, :, r * ATT_D:(r + 1) * ATT_D].astype(F32)
            o_ref[0, :, r * ATT_D:(r + 1) * ATT_D] = (
                _silu(gate) * o[r * tq:(r + 1) * tq]).astype(o_ref.dtype)


def _attention(u, n_ctx_rows):
    B, LT, _ = u.shape
    L = LT - n_ctx_rows
    R = ATT_HEADS // ATT_KV
    tq = ATT_TQ
    tk = next(t for t in ATT_TK_CANDIDATES if LT % t == 0)
    qw = R * ATT_D
    q_off = n_ctx_rows // tq
    k_col = ATT_HEADS
    v_col = ATT_HEADS + ATT_KV
    g_col = (ATT_HEADS + 2 * ATT_KV) * ATT_D // qw
    return pl.pallas_call(
        _attn_kernel,
        grid=(B, ATT_KV, L // tq, LT // tk),
        in_specs=[pl.BlockSpec((1, tq, qw), lambda b, g, i, j: (b, i + q_off, g)),
                  pl.BlockSpec((1, tk, ATT_D), lambda b, g, i, j: (b, j, k_col + g)),
                  pl.BlockSpec((1, tk, ATT_D), lambda b, g, i, j: (b, j, v_col + g)),
                  pl.BlockSpec((1, tq, qw), lambda b, g, i, j: (b, i + q_off, g_col + g))],
        out_specs=pl.BlockSpec((1, tq, qw), lambda b, g, i, j: (b, i, g)),
        out_shape=jax.ShapeDtypeStruct((B, L, ATT_HEADS * ATT_D), BF16),
        scratch_shapes=[pltpu.VMEM((R * tq, 1), F32), pltpu.VMEM((R * tq, 1), F32),
                        pltpu.VMEM((R * tq, ATT_D), F32)],
        compiler_params=_cparams(("arbitrary", "arbitrary", "arbitrary", "arbitrary")),
        name="attention",
    )(u, u, u, u)


def _rope_angles(pos, dim):
    half = dim // 2
    inv = ROPE_THETA ** (-jnp.arange(half, dtype=F32) / half)
    return pos[:, None] * inv[None, :]


def _with_ctx_identity(cos_like, sin_like, n_ctx_rows):
    ones = jnp.ones((n_ctx_rows, cos_like.shape[1]), F32)
    return (jnp.concatenate([ones, cos_like], axis=0),
            jnp.concatenate([jnp.zeros_like(ones), sin_like], axis=0))


def _layer_lru(xs, Lc, ms, norm_g, w_in, conv_w, conv_b, w_r, b_r, w_i, b_i, lam, w_out):
    n_ctx_blk = Lc // TOK_TILE
    u = _norm_proj(xs, ms, norm_g, w_in.astype(BF16), n_ctx_blk)
    ut = jnp.transpose(u, (1, 0, 2)).astype(F32)
    lat = None
    for d in (1, 0):
        wg = jnp.concatenate([w_r[d], w_i[d]], axis=-1).astype(BF16)
        bg = jnp.stack([b_r[d], b_i[d]], axis=0).astype(F32)
        lat = _lru_pass(d == 0, ut, conv_w.astype(F32), conv_b.astype(F32), wg, bg,
                        lam[d].astype(F32), Lc, latb=lat)
    z = jnp.transpose(lat, (1, 0, 2)).astype(BF16)
    return _out_proj_residual(z, w_out.astype(BF16), xs, ms, n_ctx_blk)


def _layer_ret(xs, Lc, ms, norm_g, w_in, decay_logit, w_out):
    n_ctx_blk = Lc // TOK_TILE
    L = xs.shape[1] - Lc
    ang = _rope_angles(jnp.arange(L, dtype=F32), RET_DK)
    cos, sin = _with_ctx_identity(jnp.cos(ang), jnp.sin(ang), Lc)
    u = _norm_proj(xs, ms, norm_g, w_in.astype(BF16), n_ctx_blk, "ret", (cos, sin))
    dl = decay_logit.astype(F32)
    logg = -(jnp.maximum(-dl, 0.0) + jnp.log1p(jnp.exp(-jnp.abs(dl))))
    blk = jnp.exp(RET_CHUNK * logg)
    ob = _ret_pass(False, u, logg, blk, Lc)
    z = _ret_pass(True, u, logg, blk, Lc, ob=ob)
    return _out_proj_residual(z, w_out.astype(BF16), xs, ms, n_ctx_blk)


def _layer_hyena(xs, Lc, ms, norm_g, w_in, conv_w, conv_b, fw1, fb1, fw2, fb2, fw3, freq, skip,
                 w_out):
    n_ctx_blk = Lc // TOK_TILE
    L = xs.shape[1] - Lc
    u = _norm_proj(xs, ms, norm_g, w_in.astype(BF16), n_ctx_blk)
    hy_args = (conv_w, conv_b, fw1, fb1, fw2, fb2, fw3, freq, skip)
    z = jnp.concatenate([_hy_segment(u[:, :Lc], Lc, *hy_args),
                         _hy_segment(u[:, Lc:], L, *hy_args)], axis=1)
    return _out_proj_residual(z, w_out.astype(BF16), xs, ms, n_ctx_blk)


def _layer_attn(xs, Lc, ms, norm_g, w_in, q_norm_g, k_norm_g, w_out):
    n_ctx_blk = Lc // TOK_TILE
    L = xs.shape[1] - Lc
    rows = L // GRID_W
    row = jnp.repeat(jnp.arange(rows, dtype=F32), GRID_W)
    col = jnp.tile(jnp.arange(GRID_W, dtype=F32), rows)
    ang = jnp.concatenate([_rope_angles(row, ATT_D // 2), _rope_angles(col, ATT_D // 2)], axis=-1)
    cos, sin = _with_ctx_identity(jnp.cos(ang), jnp.sin(ang), Lc)
    c2 = jnp.concatenate([cos, cos], axis=-1)
    s2 = jnp.concatenate([-sin, sin], axis=-1)
    u = _norm_proj(xs, ms, norm_g, w_in.astype(BF16), n_ctx_blk, "att",
                   (c2, s2, q_norm_g.astype(F32).reshape(1, ATT_D),
                    k_norm_g.astype(F32).reshape(1, ATT_D)))
    z = _attention(u, Lc)
    return _out_proj_residual(z, w_out.astype(BF16), xs, ms, n_ctx_blk, x_off_blk=n_ctx_blk)


def kernel(x, c, ctx, c_ctx, lru_mod_w, lru_mod_b, lru_norm_g, lru_w_in, lru_conv_w, lru_conv_b, lru_w_r, lru_b_r, lru_w_i, lru_b_i, lru_lambda, lru_w_out, ret_mod_w, ret_mod_b, ret_norm_g, ret_w_in, ret_decay_logit, ret_w_out, hy_mod_w, hy_mod_b, hy_norm_g, hy_w_in, hy_conv_w, hy_conv_b, hy_fw1, hy_fb1, hy_fw2, hy_fb2, hy_fw3, hy_freq, hy_skip, hy_w_out, att_mod_w, att_mod_b, att_norm_g, att_w_in, att_q_norm_g, att_k_norm_g, att_w_out, final_norm_g):
    B, L, D = x.shape
    Lc = ctx.shape[1]
    LT = Lc + L
    assert B == SUBLANES and B % 2 == 0
    assert Lc % TOK_TILE == 0 and L % TOK_TILE == 0 and Lc % RET_CHUNK == 0 and Lc % LRU_T == 0
    n_ctx_blk = Lc // TOK_TILE

    cvec = jnp.zeros((16, D), F32).at[:B].set(c).at[B].set(c_ctx)

    def modsel(mod_w, mod_b):
        m = _modulation(cvec, mod_w, mod_b).reshape(16, 3, D)
        return jnp.stack([jnp.broadcast_to(m[B][None], (B, 3, D)), m[:B]], axis=0)

    xs = jnp.concatenate([ctx, x], axis=1)
    xs = _layer_lru(xs, Lc, modsel(lru_mod_w, lru_mod_b), lru_norm_g, lru_w_in, lru_conv_w,
                    lru_conv_b, lru_w_r, lru_b_r, lru_w_i, lru_b_i, lru_lambda, lru_w_out)
    xs = _layer_ret(xs, Lc, modsel(ret_mod_w, ret_mod_b), ret_norm_g, ret_w_in, ret_decay_logit,
                    ret_w_out)
    xs = _layer_hyena(xs, Lc, modsel(hy_mod_w, hy_mod_b), hy_norm_g, hy_w_in, hy_conv_w, hy_conv_b,
                      hy_fw1, hy_fb1, hy_fw2, hy_fb2, hy_fw3, hy_freq, hy_skip, hy_w_out)
    xl = _layer_attn(xs, Lc, modsel(att_mod_w, att_mod_b), att_norm_g, att_w_in, att_q_norm_g,
                     att_k_norm_g, att_w_out)
    return _final_norm(xl, final_norm_g)
```

```python
import functools
import math

import numpy as np
import jax
import jax.numpy as jnp
from jax import lax
from jax.experimental import pallas as pl
from jax.experimental.pallas import tpu as pltpu

F32 = jnp.float32
BF16 = jnp.bfloat16
HIGHEST = lax.Precision.HIGHEST

NORM_EPS = 1e-6
ROPE_THETA = 10000.0
GRID_W = 64
LANES = 128
SUBLANES = 8
VMEM_LIMIT = 56 * 1024 * 1024

TOK_TILE = 256

LRU_C = 8.0
LRU_BLOCK = 128
LRU_T = 64
LRU_WC = 512

RET_HEADS = 4
RET_DK = 256
RET_DV = 512
RET_CHUNK = 256

HY_EMB = 33
HY_HIDDEN = 64
HY_FAST_DECAY = 0.3
HY_SLOW_DECAY = 1.5
HY_TARGET = 1e-2
HY_CB = 8
HY_CB_SHORT = 32
HY_SHORT_SEQ = 4096

ATT_HEADS = 8
ATT_KV = 2
ATT_D = 128
ATT_TQ = 256
ATT_CHUNK_MAX = 1408


def _cparams(sem):
    return pltpu.CompilerParams(dimension_semantics=sem, vmem_limit_bytes=VMEM_LIMIT)


def _dot(a, b, precision=None):
    return lax.dot_general(a, b, (((1,), (0,)), ((), ())), precision=precision,
                           preferred_element_type=F32)


def _dot_nt(a, b, precision=None):
    return lax.dot_general(a, b, (((1,), (1,)), ((), ())), precision=precision,
                           preferred_element_type=F32)


def _dot_tn(a, b):
    return lax.dot_general(a, b, (((0,), (0,)), ((), ())), preferred_element_type=F32)


def _silu(x):
    xh = 0.5 * x
    return xh * jnp.tanh(xh) + xh


def _mod_kernel(c_ref, w_ref, b_ref, o_ref):
    o_ref[...] = _dot(_silu(c_ref[...]), w_ref[...], HIGHEST) + b_ref[...]


def _modulation(cvec, mod_w, mod_b):
    D = cvec.shape[1]
    return pl.pallas_call(
        _mod_kernel,
        grid=(3,),
        in_specs=[pl.BlockSpec((16, D), lambda n: (0, 0)),
                  pl.BlockSpec((D, D), lambda n: (0, n)),
                  pl.BlockSpec((1, D), lambda n: (0, n))],
        out_specs=pl.BlockSpec((16, D), lambda n: (0, n)),
        out_shape=jax.ShapeDtypeStruct((16, 3 * D), F32),
        compiler_params=_cparams(("arbitrary",)),
        name="modulation",
    )(cvec, mod_w, mod_b.reshape(1, 3 * D))


def _nmm_kernel(mode, x_ref, mod_ref, g_ref, w_ref, *rest):
    o_ref = rest[-1]
    x = x_ref[0]
    ms = jnp.mean(x * x, axis=-1, keepdims=True)
    xn = x * lax.rsqrt(ms + NORM_EPS) * g_ref[...]
    mod = mod_ref[0, 0]
    h = xn * (1.0 + mod[1:2]) + mod[0:1]
    u = _dot(h.astype(BF16), w_ref[...])
    if mode == "plain":
        o_ref[0] = u.astype(o_ref.dtype)
    elif mode == "ret":
        cos = rest[0][...]
        sin = rest[1][...]
        nk = RET_HEADS * RET_DK
        half = RET_DK // 2
        for base, scale in ((0, 1.0), (nk, RET_DK ** -0.5)):
            for hh in range(RET_HEADS):
                c0 = base + hh * RET_DK
                x1 = u[:, c0:c0 + half]
                x2 = u[:, c0 + half:c0 + RET_DK]
                o_ref[0, :, c0:c0 + half] = ((x1 * cos - x2 * sin) * scale).astype(o_ref.dtype)
                o_ref[0, :, c0 + half:c0 + RET_DK] = ((x1 * sin + x2 * cos) * scale).astype(o_ref.dtype)
        o_ref[0, :, 2 * nk:] = u[:, 2 * nk:].astype(o_ref.dtype)
    elif mode == "att":
        c2 = rest[0][...]
        s2 = rest[1][...]
        qg = rest[2][...]
        kg = rest[3][...]
        nqk = ATT_HEADS + ATT_KV
        for hh in range(nqk):
            xh = u[:, hh * ATT_D:(hh + 1) * ATT_D]
            g = qg if hh < ATT_HEADS else kg
            r = lax.rsqrt(jnp.mean(xh * xh, axis=-1, keepdims=True) + NORM_EPS)
            xh = xh * r * g
            rot = xh * c2 + pltpu.roll(xh, ATT_D // 2, 1) * s2
            if hh < ATT_HEADS:
                rot = rot * (ATT_D ** -0.5 * math.log2(math.e))
            o_ref[0, :, hh * ATT_D:(hh + 1) * ATT_D] = rot.astype(o_ref.dtype)
        o_ref[0, :, nqk * ATT_D:] = u[:, nqk * ATT_D:].astype(o_ref.dtype)
    else:
        raise ValueError(mode)


def _norm_proj(x, modsel, norm_g, w_bf16, n_ctx_blk, mode="plain", extras=()):
    B, LT, D = x.shape
    F = w_bf16.shape[1]
    T = TOK_TILE
    tab_specs = []
    for e in extras:
        if e.shape[0] == 1:
            tab_specs.append(pl.BlockSpec(e.shape, lambda b, i: (0, 0)))
        else:
            tab_specs.append(pl.BlockSpec((T, e.shape[1]), lambda b, i: (i, 0)))
    return pl.pallas_call(
        functools.partial(_nmm_kernel, mode),
        grid=(B, LT // T),
        in_specs=[pl.BlockSpec((1, T, D), lambda b, i: (b, i, 0)),
                  pl.BlockSpec((1, 1, 3, D), lambda b, i: (jnp.where(i < n_ctx_blk, 0, 1), b, 0, 0)),
                  pl.BlockSpec((1, D), lambda b, i: (0, 0)),
                  pl.BlockSpec((D, F), lambda b, i: (0, 0), pipeline_mode=pl.Buffered(1)),
                  ] + tab_specs,
        out_specs=pl.BlockSpec((1, T, F), lambda b, i: (b, i, 0)),
        out_shape=jax.ShapeDtypeStruct((B, LT, F), BF16),
        compiler_params=_cparams(("arbitrary", "arbitrary")),
        name="norm_proj_" + mode,
    )(x, modsel, norm_g.reshape(1, D), w_bf16, *extras)


def _opr_kernel(z_ref, w_ref, x_ref, mod_ref, o_ref):
    y = _dot(z_ref[0], w_ref[...])
    o_ref[0] = x_ref[0] + mod_ref[0, 0][2:3] * y


def _out_proj_residual(z, w_bf16, x, modsel, n_ctx_blk, x_off_blk=0):
    B, Lz, Fz = z.shape
    D = x.shape[2]
    T = TOK_TILE
    return pl.pallas_call(
        _opr_kernel,
        grid=(B, Lz // T),
        in_specs=[pl.BlockSpec((1, T, Fz), lambda b, i: (b, i, 0)),
                  pl.BlockSpec((Fz, D), lambda b, i: (0, 0), pipeline_mode=pl.Buffered(1)),
                  pl.BlockSpec((1, T, D), lambda b, i: (b, i + x_off_blk, 0)),
                  pl.BlockSpec((1, 1, 3, D),
                               lambda b, i: (jnp.where(i + x_off_blk < n_ctx_blk, 0, 1), b, 0, 0))],
        out_specs=pl.BlockSpec((1, T, D), lambda b, i: (b, i, 0)),
        out_shape=jax.ShapeDtypeStruct((B, Lz, D), F32),
        compiler_params=_cparams(("arbitrary", "arbitrary")),
        name="out_proj_residual",
    )(z, w_bf16, x, modsel)


def _final_norm_kernel(x_ref, g_ref, o_ref):
    x = x_ref[0]
    ms = jnp.mean(x * x, axis=-1, keepdims=True)
    o_ref[0] = x * lax.rsqrt(ms + NORM_EPS) * g_ref[...]


def _final_norm(x, g):
    B, L, D = x.shape
    T = TOK_TILE
    return pl.pallas_call(
        _final_norm_kernel,
        grid=(B, L // T),
        in_specs=[pl.BlockSpec((1, T, D), lambda b, i: (b, i, 0)),
                  pl.BlockSpec((1, D), lambda b, i: (0, 0))],
        out_specs=pl.BlockSpec((1, T, D), lambda b, i: (b, i, 0)),
        out_shape=jax.ShapeDtypeStruct((B, L, D), F32),
        compiler_params=_cparams(("arbitrary", "arbitrary")),
        name="final_norm",
    )(x, g.reshape(1, D))


def _lru_kernel(fwd, T, n_ctx, n_tot, u_ref, up_ref, un_ref, cw_ref, cb_ref, wg_ref, bg_ref,
                lam_ref, *rest):
    if fwd:
        gate_ref, latb_ref, o_ref, h_sc, ext_sc, hs_sc = rest
    else:
        o_ref, h_sc, ext_sc = rest
    Wc = u_ref.shape[2]
    NB = SUBLANES
    HALO = up_ref.shape[1]
    s = pl.program_id(1)
    if fwd:
        c = s
    else:
        c = jnp.where(s < n_ctx, n_ctx - 1 - s, n_tot + n_ctx - 1 - s)

    @pl.when(s == 0)
    def _():
        h_sc[...] = jnp.zeros_like(h_sc)

    first = jnp.logical_or(c == 0, c == n_ctx)
    last = jnp.logical_or(c == n_ctx - 1, c == n_tot - 1)
    pm = jnp.where(first, 0.0, 1.0)
    nm = jnp.where(last, 0.0, 1.0)
    for b in range(NB):
        prev = up_ref[b, HALO - 2:HALO, :].astype(F32) * pm
        nxt = un_ref[b, 0:1, :].astype(F32) * nm
        for n in range(Wc // LRU_BLOCK):
            sl = slice(n * LRU_BLOCK, (n + 1) * LRU_BLOCK)
            ext_sc[n, pl.ds(b, 1), :] = prev[0:1, sl]
            ext_sc[n, pl.ds(NB + b, 1), :] = prev[1:2, sl]
            ext_sc[n, pl.ds(2 * NB + b, T, stride=NB), :] = u_ref[b, :, sl].astype(F32)
            ext_sc[n, pl.ds((T + 2) * NB + b, 1), :] = nxt[:, sl]
    lam = lam_ref[...]
    sp = jnp.maximum(-lam, 0.0) + jnp.log1p(jnp.exp(-jnp.abs(lam)))
    c2 = (-0.5 * LRU_C * math.log2(math.e)) * sp
    for n in range(Wc // LRU_BLOCK):
        sl = slice(n * LRU_BLOCK, (n + 1) * LRU_BLOCK)
        xn = cb_ref[:, sl] + cw_ref[0:1, sl] * ext_sc[n, 0:T * NB, :]
        for k in range(1, 4):
            xn = xn + cw_ref[k:k + 1, sl] * ext_sc[n, k * NB:(k + T) * NB, :]
        g = _dot(xn.astype(BF16), wg_ref[n])
        tr = jnp.tanh(g[:, :LRU_BLOCK] + bg_ref[0:1, sl])
        ti = jnp.tanh(g[:, LRU_BLOCK:] + bg_ref[1:2, sl])
        a = jnp.exp2(c2[:, sl] * tr + c2[:, sl])
        y = 1.0 - a * a
        xh = 0.5 * xn
        bb = jnp.where(y > 0.0, y * lax.rsqrt(y), 0.0) * (xh * ti + xh)
        a3 = a.reshape(T, SUBLANES, LRU_BLOCK)
        b3 = bb.reshape(T, SUBLANES, LRU_BLOCK)
        h = h_sc[:, sl]
        for t in (range(T) if fwd else range(T - 1, -1, -1)):
            h = a3[t] * h + b3[t]
            if fwd:
                hs_sc[n, t * NB:(t + 1) * NB, :] = h + latb_ref[t, :, sl]
            else:
                o_ref[t, :, sl] = h
        h_sc[:, sl] = h
        if fwd:
            for b in range(NB):
                hb = hs_sc[n, pl.ds(b, T, stride=NB), :]
                o_ref[b, :, sl] = (hb * _silu(gate_ref[b, :, sl].astype(F32))).astype(o_ref.dtype)


def _lru_pass(fwd, u, conv_w, conv_b, wg, bg, lam, n_ctx_rows, latb=None):
    B, LT, W2 = u.shape
    W = W2 // 2
    T, Wc = LRU_T, LRU_WC
    n_tot = LT // T
    n_ctx = n_ctx_rows // T
    nj = W // Wc
    halo = 2 * SUBLANES
    hb = T // halo

    def chunk(s):
        if fwd:
            return s
        return jnp.where(s < n_ctx, n_ctx - 1 - s, n_tot + n_ctx - 1 - s)

    in_specs = [
        pl.BlockSpec((B, T, Wc), lambda j, s: (0, chunk(s), j)),
        pl.BlockSpec((B, halo, Wc), lambda j, s: (0, jnp.maximum(chunk(s) * hb - 1, 0), j)),
        pl.BlockSpec((B, halo, Wc),
                     lambda j, s: (0, jnp.minimum((chunk(s) + 1) * hb, LT // halo - 1), j)),
        pl.BlockSpec((4, Wc), lambda j, s: (0, j)),
        pl.BlockSpec((1, Wc), lambda j, s: (0, j)),
        pl.BlockSpec((Wc // LRU_BLOCK, LRU_BLOCK, 2 * LRU_BLOCK), lambda j, s: (j, 0, 0)),
        pl.BlockSpec((2, Wc), lambda j, s: (0, j)),
        pl.BlockSpec((1, Wc), lambda j, s: (0, j)),
    ]
    args = [u, u, u, conv_w, conv_b.reshape(1, W), wg, bg, lam.reshape(1, W)]
    nb = Wc // LRU_BLOCK
    scratch = [pltpu.VMEM((B, Wc), F32), pltpu.VMEM((nb, (T + 3) * B, LRU_BLOCK), F32)]
    if fwd:
        in_specs += [pl.BlockSpec((B, T, Wc), lambda j, s: (0, s, nj + j)),
                     pl.BlockSpec((T, B, Wc), lambda j, s: (s, 0, j))]
        args += [u, latb]
        scratch.append(pltpu.VMEM((nb, T * B, LRU_BLOCK), F32))
        out_specs = pl.BlockSpec((B, T, Wc), lambda j, s: (0, s, j))
        out_shape = jax.ShapeDtypeStruct((B, LT, W), BF16)
    else:
        out_specs = pl.BlockSpec((T, B, Wc), lambda j, s: (chunk(s), 0, j))
        out_shape = jax.ShapeDtypeStruct((LT, B, W), F32)
    return pl.pallas_call(
        functools.partial(_lru_kernel, fwd, T, n_ctx, n_tot),
        grid=(nj, n_tot),
        in_specs=in_specs,
        out_specs=out_specs,
        out_shape=out_shape,
        scratch_shapes=scratch,
        compiler_params=_cparams(("arbitrary", "arbitrary")),
        name="lru_fwd" if fwd else "lru_bwd",
    )(*args)


def _ret_kernel(fwd, C, lg_ref, blk_ref, q_ref, k_ref, v_ref, *rest):
    if fwd:
        gate_ref, ob_ref, o_ref, st_sc = rest
    else:
        o_ref, st_sc = rest
    s = pl.program_id(1)

    @pl.when(s == 0)
    def _():
        st_sc[...] = jnp.zeros_like(st_sc)

    d = 0 if fwd else 1
    row_v = lax.broadcasted_iota(jnp.int32, (C, RET_DV), 0).astype(F32)
    row_k = lax.broadcasted_iota(jnp.int32, (C, RET_DK), 0).astype(F32)
    if fwd:
        ri = lax.broadcasted_iota(jnp.int32, (C, C), 0)
        ci = lax.broadcasted_iota(jnp.int32, (C, C), 1)
        diff = (ri - ci).astype(F32)
    for hh in range(RET_HEADS):
        lg = lg_ref[d, hh]
        qh = q_ref[0, :, hh * RET_DK:(hh + 1) * RET_DK]
        kh = k_ref[0, :, hh * RET_DK:(hh + 1) * RET_DK]
        vh = v_ref[0, :, hh * RET_DV:(hh + 1) * RET_DV]
        if fwd:
            q_dec = jnp.exp((row_v + 1.0) * lg)
            k_dec = jnp.exp((C - 1.0 - row_k) * lg)
        else:
            q_dec = jnp.exp((C - row_v) * lg)
            k_dec = jnp.exp(row_k * lg)
        st = st_sc[hh]
        o = _dot(qh, st.astype(BF16)) * q_dec
        kd = (kh.astype(F32) * k_dec).astype(BF16)
        st_sc[hh] = st * blk_ref[d, hh] + _dot_tn(kd, vh)
        if fwd:
            mask = jnp.exp(jnp.where(diff >= 0, diff * lg_ref[0, hh], -diff * lg_ref[1, hh]))
            sc = _dot_nt(qh, kh) * mask
            o = o + _dot(sc.astype(BF16), vh)
            o = o + ob_ref[0, :, hh * RET_DV:(hh + 1) * RET_DV].astype(F32)
            o = o * lax.rsqrt(jnp.mean(o * o, axis=-1, keepdims=True) + NORM_EPS)
            gate = gate_ref[0, :, hh * RET_DV:(hh + 1) * RET_DV].astype(F32)
            o_ref[0, :, hh * RET_DV:(hh + 1) * RET_DV] = (_silu(gate) * o).astype(o_ref.dtype)
        else:
            o_ref[0, :, hh * RET_DV:(hh + 1) * RET_DV] = o.astype(o_ref.dtype)


def _ret_pass(fwd, u, logg, blk, n_ctx_rows, ob=None):
    B, LT, _ = u.shape
    C = RET_CHUNK
    nk = RET_HEADS * RET_DK
    nv = RET_HEADS * RET_DV
    n_tot = LT // C
    n_ctx = n_ctx_rows // C

    def chunk(s):
        if fwd:
            return s
        return jnp.where(s < n_ctx, n_ctx - 1 - s, n_tot + n_ctx - 1 - s)

    smem = pl.BlockSpec(memory_space=pltpu.SMEM)
    in_specs = [smem, smem,
                pl.BlockSpec((1, C, nk), lambda b, s: (b, chunk(s), 0)),
                pl.BlockSpec((1, C, nk), lambda b, s: (b, chunk(s), 1)),
                pl.BlockSpec((1, C, nv), lambda b, s: (b, chunk(s), 1))]
    args = [logg, blk, u, u, u]
    if fwd:
        in_specs += [pl.BlockSpec((1, C, nv), lambda b, s: (b, s, 2)),
                     pl.BlockSpec((1, C, nv), lambda b, s: (b, s, 0))]
        args += [u, ob]
    return pl.pallas_call(
        functools.partial(_ret_kernel, fwd, C),
        grid=(B, n_tot),
        in_specs=in_specs,
        out_specs=pl.BlockSpec((1, C, nv), lambda b, s: (b, chunk(s), 0)),
        out_shape=jax.ShapeDtypeStruct((B, LT, nv), BF16),
        scratch_shapes=[pltpu.VMEM((RET_HEADS, RET_DK, RET_DV), F32)],
        compiler_params=_cparams(("arbitrary", "arbitrary")),
        name="ret_fwd" if fwd else "ret_bwd",
    )(*args)


def _hy_filter_kernel(z_ref, tt_ref, fw1_ref, fb1_ref, fw2_ref, fb2_ref, fq_ref, w3_ref,
                      delta_ref, o_ref):
    fq = fq_ref[...]
    h1 = jnp.sin(fq * (_dot(z_ref[...], fw1_ref[...], HIGHEST) + fb1_ref[...]))
    h2 = jnp.sin(fq * (_dot(h1, fw2_ref[...], HIGHEST) + fb2_ref[...]))
    kt = _dot_nt(w3_ref[0], h2, HIGHEST)
    t = tt_ref[0, 0:1, :]
    msk = tt_ref[0, 1:2, :]
    o_ref[0] = kt * jnp.exp(-delta_ref[...] * t) * msk


def _hy_positions(Lv, R):
    N = R * LANES
    p = np.arange(N)
    fwd = p < Lv
    bwd = p > N - Lv
    tap = np.where(fwd, p, np.where(bwd, N - p, 0))
    return tap.astype(np.int32), (fwd | bwd).astype(np.float32)


def _hy_filter_taps(Lv, R, fw1, fb1, fw2, fb2, fw3, freq):
    W = fw3.shape[1] // 4
    bands = (HY_EMB - 1) // 2
    t = jnp.linspace(0.0, 1.0, Lv, dtype=F32)[:, None]
    w = 2.0 * math.pi * jnp.arange(Lv, dtype=F32)[:, None] / Lv
    fr = jnp.linspace(1e-4, bands - 1, bands, dtype=F32)[None, :]
    z = jnp.concatenate([t, jnp.cos(fr * w), -jnp.sin(fr * w)], axis=-1)
    tap, valid = _hy_positions(Lv, R)
    zp = jnp.pad(z[tap], ((0, 0), (0, LANES - HY_EMB)))
    tt = jnp.stack([t[tap, 0].reshape(R, LANES), jnp.asarray(valid).reshape(R, LANES)], axis=1)
    fw1p = jnp.pad(fw1.astype(F32), ((0, LANES - HY_EMB), (0, 0)))
    w3 = fw3.astype(F32).reshape(HY_HIDDEN, 2, 2, W)
    w3d = jnp.stack([w3[:, :, 0, :].reshape(HY_HIDDEN, 2 * W).T,
                     w3[:, :, 1, :].reshape(HY_HIDDEN, 2 * W).T], axis=0)
    deltas = jnp.abs(jnp.linspace(math.log(HY_TARGET) / HY_SLOW_DECAY,
                                  math.log(HY_TARGET) / HY_FAST_DECAY, W, dtype=F32))
    delta2 = jnp.concatenate([deltas, deltas]).reshape(2 * W, 1)
    F2 = 2 * W
    half = R // 2
    kt = pl.pallas_call(
        _hy_filter_kernel,
        grid=(R,),
        in_specs=[pl.BlockSpec((LANES, LANES), lambda r: (r, 0)),
                  pl.BlockSpec((1, 2, LANES), lambda r: (r, 0, 0)),
                  pl.BlockSpec((LANES, HY_HIDDEN), lambda r: (0, 0)),
                  pl.BlockSpec((1, HY_HIDDEN), lambda r: (0, 0)),
                  pl.BlockSpec((HY_HIDDEN, HY_HIDDEN), lambda r: (0, 0)),
                  pl.BlockSpec((1, HY_HIDDEN), lambda r: (0, 0)),
                  pl.BlockSpec((1, HY_HIDDEN), lambda r: (0, 0)),
                  pl.BlockSpec((1, F2, HY_HIDDEN), lambda r: (jnp.where(r < half, 0, 1), 0, 0)),
                  pl.BlockSpec((F2, 1), lambda r: (0, 0))],
        out_specs=pl.BlockSpec((1, F2, LANES), lambda r: (r, 0, 0)),
        out_shape=jax.ShapeDtypeStruct((R, F2, LANES), F32),
        compiler_params=_cparams(("arbitrary",)),
        name="hyena_filter_taps",
    )(zp, tt, fw1p, fb1.astype(F32).reshape(1, -1), fw2.astype(F32), fb2.astype(F32).reshape(1, -1),
      freq.astype(F32).reshape(1, -1), w3d, delta2)
    return jnp.transpose(kt, (1, 0, 2))


def _dft_constants(R0, R):
    N = R * LANES
    k1 = np.arange(R)[:, None]
    n1 = np.arange(R0)[None, :]
    f1 = np.exp(-2j * np.pi * k1 * n1 / R)
    ma = np.block([[f1.real, -f1.imag], [f1.imag, f1.real]])
    g1 = np.exp(2j * np.pi * n1.T * k1.T / R) / N
    mai = np.block([[g1.real, -g1.imag], [g1.imag, g1.real]])
    n2 = np.arange(LANES)
    tw = np.exp(-2j * np.pi * np.arange(R)[:, None] * n2[None, :] / N)
    f2 = np.exp(-2j * np.pi * n2[:, None] * n2[None, :] / LANES)
    mc = np.block([[f2.real, f2.imag], [-f2.imag, f2.real]])
    mci = np.block([[f2.real, -f2.imag], [f2.imag, f2.real]])
    f1_full = np.exp(-2j * np.pi * k1 * np.arange(R)[None, :] / R)
    as_bf = lambda m: jnp.asarray(m, F32).astype(BF16)
    return dict(ma=as_bf(ma), mai=as_bf(mai), mc=as_bf(mc), mci=as_bf(mci),
                tr=jnp.asarray(tw.real, F32), ti=jnp.asarray(tw.imag, F32),
                ma_real=as_bf(np.concatenate([f1_full.real, f1_full.imag], axis=0)))


def _hy_fft_kernel(R, cb, k_ref, ma_ref, tr_ref, ti_ref, mc_ref, or_ref, oi_ref):
    cols = [k_ref[c] for c in range(cb)]
    inv = []
    for col in cols:
        sabs = jnp.sum(jnp.sum(jnp.abs(col), axis=1, keepdims=True), axis=0, keepdims=True)
        inv.append(1.0 / sabs)
    rhs = jnp.concatenate(cols, axis=1)
    hi = rhs.astype(BF16)
    lo = (rhs - hi.astype(F32)).astype(BF16)
    a = _dot(ma_ref[...], hi) + _dot(ma_ref[...], lo)
    tr = tr_ref[...]
    ti = ti_ref[...]
    rows = []
    for c in range(cb):
        ar = a[:R, c * LANES:(c + 1) * LANES]
        ai = a[R:, c * LANES:(c + 1) * LANES]
        rows.append(jnp.concatenate([ar * tr - ai * ti, ar * ti + ai * tr], axis=1))
    lhs = jnp.concatenate(rows, axis=0)
    hi = lhs.astype(BF16)
    lo = (lhs - hi.astype(F32)).astype(BF16)
    x = _dot(hi, mc_ref[...]) + _dot(lo, mc_ref[...])
    for c in range(cb):
        or_ref[c] = x[c * R:(c + 1) * R, :LANES] * inv[c]
        oi_ref[c] = x[c * R:(c + 1) * R, LANES:] * inv[c]


def _hy_filter_spectrum(kt, R, consts):
    F2 = kt.shape[0]
    cb = HY_CB
    spec = pl.BlockSpec((cb, R, LANES), lambda j: (j, 0, 0))
    full = lambda a: pl.BlockSpec(a.shape, lambda j: (0,) * a.ndim)
    return pl.pallas_call(
        functools.partial(_hy_fft_kernel, R, cb),
        grid=(F2 // cb,),
        in_specs=[spec, full(consts["ma_real"]), full(consts["tr"]), full(consts["ti"]),
                  full(consts["mc"])],
        out_specs=[spec, spec],
        out_shape=[jax.ShapeDtypeStruct((F2, R, LANES), F32)] * 2,
        compiler_params=_cparams(("arbitrary",)),
        name="hyena_filter_fft",
    )(kt, consts["ma_real"], consts["tr"], consts["ti"], consts["mc"])


def _hy_conv_kernel(R0, R, Lv, cb, W, cw_ref, cbias_ref, skip_ref, v_ref, x1_ref, x2_ref, g_ref,
                    k0r_ref, k0i_ref, k1r_ref, k1i_ref, ma_ref, mai_ref, mc_ref, mci_ref,
                    tr_ref, ti_ref, o_ref):
    j = pl.program_id(0)
    row = lax.broadcasted_iota(jnp.int32, (R0, LANES), 0)
    lane = lax.broadcasted_iota(jnp.int32, (R0, LANES), 1)
    pos = row * LANES + lane
    tr = tr_ref[...]
    ti = ti_ref[...]

    def short_conv(ref, part, c, grp):
        x = ref[0, part, c].astype(F32)
        ch = grp * W + j * cb + c
        r = pltpu.roll(x, 1, 1)
        prev = jnp.where(lane == 0, pltpu.roll(r, 1, 0), r)
        prev = jnp.where(pos == 0, 0.0, prev)
        r = pltpu.roll(x, LANES - 1, 1)
        nxt = jnp.where(lane == LANES - 1, pltpu.roll(r, R0 - 1, 0), r)
        nxt = jnp.where(pos == R0 * LANES - 1, 0.0, nxt)
        y = cw_ref[0, ch] * prev + cw_ref[1, ch] * x + cw_ref[2, ch] * nxt + cbias_ref[ch]
        if Lv < R0 * LANES:
            y = jnp.where(pos < Lv, y, 0.0)
        return y

    def long_conv(re, im, kr_ref, ki_ref):
        rhs = jnp.concatenate([jnp.concatenate([re[c], im[c]], axis=0) for c in range(cb)], axis=1)
        a = _dot(ma_ref[...], rhs.astype(BF16))
        rows = []
        for c in range(cb):
            ar = a[:R, c * LANES:(c + 1) * LANES]
            ai = a[R:, c * LANES:(c + 1) * LANES]
            rows.append(jnp.concatenate([ar * tr - ai * ti, ar * ti + ai * tr], axis=1))
        x = _dot(jnp.concatenate(rows, axis=0).astype(BF16), mc_ref[...])
        rows = []
        for c in range(cb):
            xr = x[c * R:(c + 1) * R, :LANES]
            xi = x[c * R:(c + 1) * R, LANES:]
            kr = kr_ref[c]
            ki = ki_ref[c]
            rows.append(jnp.concatenate([xr * kr - xi * ki, xr * ki + xi * kr], axis=1))
        b = _dot(jnp.concatenate(rows, axis=0).astype(BF16), mci_ref[...])
        cols = []
        for c in range(cb):
            br = b[c * R:(c + 1) * R, :LANES]
            bi = b[c * R:(c + 1) * R, LANES:]
            cols.append(jnp.concatenate([br * tr + bi * ti, bi * tr - br * ti], axis=0))
        y = _dot(mai_ref[...], jnp.concatenate(cols, axis=1).astype(BF16))
        out_re = [y[:R0, c * LANES:(c + 1) * LANES] for c in range(cb)]
        out_im = [y[R0:, c * LANES:(c + 1) * LANES] for c in range(cb)]
        return out_re, out_im

    v = [[short_conv(v_ref, p, c, 0) for c in range(cb)] for p in range(2)]
    x1 = [[short_conv(x1_ref, p, c, 1) for c in range(cb)] for p in range(2)]
    cr, ci = long_conv(v[0], v[1], k0r_ref, k0i_ref)
    conv = (cr, ci)
    y = [[x1[p][c] * (conv[p][c] + v[p][c] * skip_ref[0, j * cb + c]) for c in range(cb)]
         for p in range(2)]
    x2 = [[short_conv(x2_ref, p, c, 2) for c in range(cb)] for p in range(2)]
    cr, ci = long_conv(y[0], y[1], k1r_ref, k1i_ref)
    conv = (cr, ci)
    for p in range(2):
        for c in range(cb):
            y2 = x2[p][c] * (conv[p][c] + y[p][c] * skip_ref[1, j * cb + c])
            gate = g_ref[0, p, c].astype(F32)
            o_ref[0, p, c] = (y2 * _silu(gate)).astype(o_ref.dtype)


def _hy_conv(ut, kfr, kfi, conv_w, conv_b, skip, Lv, R0, R, consts):
    G, _, W4, _, _ = ut.shape
    W = W4 // 4
    cb = HY_CB if R0 * LANES >= HY_SHORT_SEQ else HY_CB_SHORT
    nj = W // cb
    smem = pl.BlockSpec(memory_space=pltpu.SMEM)
    full = lambda a: pl.BlockSpec(a.shape, lambda j, g: (0,) * a.ndim)

    def data(grp):
        return pl.BlockSpec((1, 2, cb, R0, LANES), lambda j, g: (g, 0, grp * nj + j, 0, 0))

    def filt(order):
        return pl.BlockSpec((cb, R, LANES), lambda j, g: (order * nj + j, 0, 0))

    return pl.pallas_call(
        functools.partial(_hy_conv_kernel, R0, R, Lv, cb, W),
        grid=(nj, G),
        in_specs=[smem, smem, smem, data(0), data(1), data(2), data(3),
                  filt(0), filt(0), filt(1), filt(1),
                  full(consts["ma"]), full(consts["mai"]), full(consts["mc"]), full(consts["mci"]),
                  full(consts["tr"]), full(consts["ti"])],
        out_specs=pl.BlockSpec((1, 2, cb, R0, LANES), lambda j, g: (g, 0, j, 0, 0)),
        out_shape=jax.ShapeDtypeStruct((G, 2, W, R0, LANES), BF16),
        compiler_params=_cparams(("arbitrary", "arbitrary")),
        name="hyena_conv",
    )(conv_w, conv_b, skip, ut, ut, ut, ut, kfr, kfi, kfr, kfi,
      consts["ma"], consts["mai"], consts["mc"], consts["mci"], consts["tr"], consts["ti"])


def _hy_segment(u_seg, Lv, conv_w, conv_b, fw1, fb1, fw2, fb2, fw3, freq, skip):
    B, _, W4 = u_seg.shape
    W = W4 // 4
    rows = -(-Lv // LANES)
    R0 = -(-rows // SUBLANES) * SUBLANES
    R = max(R0, -(-(2 * rows) // SUBLANES) * SUBLANES)
    consts = _dft_constants(R0, R)
    kt = _hy_filter_taps(Lv, R, fw1, fb1, fw2, fb2, fw3, freq)
    kfr, kfi = _hy_filter_spectrum(kt, R, consts)
    assert Lv == rows * LANES
    ut = jnp.transpose(u_seg.reshape(B, rows, LANES, W4), (0, 3, 1, 2))
    if R0 > rows:
        ut = jnp.pad(ut, ((0, 0), (0, 0), (0, R0 - rows), (0, 0)))
    ut = ut.reshape(B // 2, 2, W4, R0, LANES)
    z = _hy_conv(ut, kfr, kfi, conv_w.astype(F32), conv_b.astype(F32), skip.astype(F32),
                 Lv, R0, R, consts)
    z = z.reshape(B, W, R0, LANES)[:, :, :rows]
    return jnp.transpose(z, (0, 2, 3, 1)).reshape(B, Lv, W)


def _attn_kernel(n_chunks, ck, q_ref, k_ref, v_ref, g_ref, o_ref, s_sc):
    R = ATT_HEADS // ATT_KV
    tq = q_ref.shape[1]
    q = jnp.concatenate([q_ref[0, :, r * ATT_D:(r + 1) * ATT_D] for r in range(R)], axis=0)
    m = None
    for c in range(n_chunks):
        s = _dot_nt(q, k_ref[0, c * ck:(c + 1) * ck, :])
        s_sc[:, c * ck:(c + 1) * ck] = s
        mc = jnp.max(s, axis=-1, keepdims=True)
        m = mc if m is None else jnp.maximum(m, mc)
    l = None
    acc = None
    for c in range(n_chunks):
        p = jnp.exp2(s_sc[:, c * ck:(c + 1) * ck] - m)
        lc = jnp.sum(p, axis=-1, keepdims=True)
        pv = _dot(p.astype(BF16), v_ref[0, c * ck:(c + 1) * ck, :])
        l = lc if l is None else l + lc
        acc = pv if acc is None else acc + pv
    o = acc / l
    for r in range(R):
        gate = g_ref[0, :, r * ATT_D:(r + 1) * ATT_D].astype(F32)
        o_ref[0, :, r * ATT_D:(r + 1) * ATT_D] = (
            _silu(gate) * o[r * tq:(r + 1) * tq]).astype(o_ref.dtype)


def _attention(u, n_ctx_rows):
    B, LT, _ = u.shape
    L = LT - n_ctx_rows
    R = ATT_HEADS // ATT_KV
    tq = ATT_TQ
    ck = max(t for t in range(LANES, ATT_CHUNK_MAX + 1, LANES) if LT % t == 0)
    qw = R * ATT_D
    q_off = n_ctx_rows // tq
    k_col = ATT_HEADS
    v_col = ATT_HEADS + ATT_KV
    g_col = (ATT_HEADS + 2 * ATT_KV) * ATT_D // qw
    return pl.pallas_call(
        functools.partial(_attn_kernel, LT // ck, ck),
        grid=(B, ATT_KV, L // tq),
        in_specs=[pl.BlockSpec((1, tq, qw), lambda b, g, i: (b, i + q_off, g)),
                  pl.BlockSpec((1, LT, ATT_D), lambda b, g, i: (b, 0, k_col + g)),
                  pl.BlockSpec((1, LT, ATT_D), lambda b, g, i: (b, 0, v_col + g)),
                  pl.BlockSpec((1, tq, qw), lambda b, g, i: (b, i + q_off, g_col + g))],
        out_specs=pl.BlockSpec((1, tq, qw), lambda b, g, i: (b, i, g)),
        out_shape=jax.ShapeDtypeStruct((B, L, ATT_HEADS * ATT_D), BF16),
        scratch_shapes=[pltpu.VMEM((R * tq, LT), F32)],
        compiler_params=_cparams(("arbitrary", "arbitrary", "arbitrary")),
        name="attention",
    )(u, u, u, u)


def _rope_angles(pos, dim):
    half = dim // 2
    inv = ROPE_THETA ** (-jnp.arange(half, dtype=F32) / half)
    return pos[:, None] * inv[None, :]


def _with_ctx_identity(cos_like, sin_like, n_ctx_rows):
    ones = jnp.ones((n_ctx_rows, cos_like.shape[1]), F32)
    return (jnp.concatenate([ones, cos_like], axis=0),
            jnp.concatenate([jnp.zeros_like(ones), sin_like], axis=0))


def _layer_lru(xs, Lc, ms, norm_g, w_in, conv_w, conv_b, w_r, b_r, w_i, b_i, lam, w_out):
    n_ctx_blk = Lc // TOK_TILE
    u = _norm_proj(xs, ms, norm_g, w_in.astype(BF16), n_ctx_blk)
    lat = None
    for d in (1, 0):
        wg = (0.5 * jnp.concatenate([w_r[d], w_i[d]], axis=-1)).astype(BF16)
        bg = 0.5 * jnp.stack([b_r[d], b_i[d]], axis=0).astype(F32)
        lat = _lru_pass(d == 0, u, conv_w.astype(F32), conv_b.astype(F32), wg, bg,
                        lam[d].astype(F32), Lc, latb=lat)
    return _out_proj_residual(lat, w_out.astype(BF16), xs, ms, n_ctx_blk)


def _layer_ret(xs, Lc, ms, norm_g, w_in, decay_logit, w_out):
    n_ctx_blk = Lc // TOK_TILE
    L = xs.shape[1] - Lc
    ang = _rope_angles(jnp.arange(L, dtype=F32), RET_DK)
    cos, sin = _with_ctx_identity(jnp.cos(ang), jnp.sin(ang), Lc)
    u = _norm_proj(xs, ms, norm_g, w_in.astype(BF16), n_ctx_blk, "ret", (cos, sin))
    dl = decay_logit.astype(F32)
    logg = -(jnp.maximum(-dl, 0.0) + jnp.log1p(jnp.exp(-jnp.abs(dl))))
    blk = jnp.exp(RET_CHUNK * logg)
    ob = _ret_pass(False, u, logg, blk, Lc)
    z = _ret_pass(True, u, logg, blk, Lc, ob=ob)
    return _out_proj_residual(z, w_out.astype(BF16), xs, ms, n_ctx_blk)


def _layer_hyena(xs, Lc, ms, norm_g, w_in, conv_w, conv_b, fw1, fb1, fw2, fb2, fw3, freq, skip,
                 w_out):
    n_ctx_blk = Lc // TOK_TILE
    L = xs.shape[1] - Lc
    u = _norm_proj(xs, ms, norm_g, w_in.astype(BF16), n_ctx_blk)
    hy_args = (conv_w, conv_b, fw1, fb1, fw2, fb2, fw3, freq, skip)
    z = jnp.concatenate([_hy_segment(u[:, :Lc], Lc, *hy_args),
                         _hy_segment(u[:, Lc:], L, *hy_args)], axis=1)
    return _out_proj_residual(z, w_out.astype(BF16), xs, ms, n_ctx_blk)


def _layer_attn(xs, Lc, ms, norm_g, w_in, q_norm_g, k_norm_g, w_out):
    n_ctx_blk = Lc // TOK_TILE
    L = xs.shape[1] - Lc
    rows = L // GRID_W
    row = jnp.repeat(jnp.arange(rows, dtype=F32), GRID_W)
    col = jnp.tile(jnp.arange(GRID_W, dtype=F32), rows)
    ang = jnp.concatenate([_rope_angles(row, ATT_D // 2), _rope_angles(col, ATT_D // 2)], axis=-1)
    cos, sin = _with_ctx_identity(jnp.cos(ang), jnp.sin(ang), Lc)
    c2 = jnp.concatenate([cos, cos], axis=-1)
    s2 = jnp.concatenate([-sin, sin], axis=-1)
    u = _norm_proj(xs, ms, norm_g, w_in.astype(BF16), n_ctx_blk, "att",
                   (c2, s2, q_norm_g.astype(F32).reshape(1, ATT_D),
                    k_norm_g.astype(F32).reshape(1, ATT_D)))
    z = _attention(u, Lc)
    return _out_proj_residual(z, w_out.astype(BF16), xs, ms, n_ctx_blk, x_off_blk=n_ctx_blk)


def kernel(x, c, ctx, c_ctx, lru_mod_w, lru_mod_b, lru_norm_g, lru_w_in, lru_conv_w, lru_conv_b, lru_w_r, lru_b_r, lru_w_i, lru_b_i, lru_lambda, lru_w_out, ret_mod_w, ret_mod_b, ret_norm_g, ret_w_in, ret_decay_logit, ret_w_out, hy_mod_w, hy_mod_b, hy_norm_g, hy_w_in, hy_conv_w, hy_conv_b, hy_fw1, hy_fb1, hy_fw2, hy_fb2, hy_fw3, hy_freq, hy_skip, hy_w_out, att_mod_w, att_mod_b, att_norm_g, att_w_in, att_q_norm_g, att_k_norm_g, att_w_out, final_norm_g):
    B, L, D = x.shape
    Lc = ctx.shape[1]
    LT = Lc + L
    assert B == SUBLANES and B % 2 == 0
    assert Lc % TOK_TILE == 0 and L % TOK_TILE == 0 and Lc % RET_CHUNK == 0 and Lc % LRU_T == 0
    n_ctx_blk = Lc // TOK_TILE

    cvec = jnp.zeros((16, D), F32).at[:B].set(c).at[B].set(c_ctx)

    def modsel(mod_w, mod_b):
        m = _modulation(cvec, mod_w, mod_b).reshape(16, 3, D)
        return jnp.stack([jnp.broadcast_to(m[B][None], (B, 3, D)), m[:B]], axis=0)

    xs = jnp.concatenate([ctx, x], axis=1)
    xs = _layer_lru(xs, Lc, modsel(lru_mod_w, lru_mod_b), lru_norm_g, lru_w_in, lru_conv_w,
                    lru_conv_b, lru_w_r, lru_b_r, lru_w_i, lru_b_i, lru_lambda, lru_w_out)
    xs = _layer_ret(xs, Lc, modsel(ret_mod_w, ret_mod_b), ret_norm_g, ret_w_in, ret_decay_logit,
                    ret_w_out)
    xs = _layer_hyena(xs, Lc, modsel(hy_mod_w, hy_mod_b), hy_norm_g, hy_w_in, hy_conv_w, hy_conv_b,
                      hy_fw1, hy_fb1, hy_fw2, hy_fb2, hy_fw3, hy_freq, hy_skip, hy_w_out)
    xl = _layer_attn(xs, Lc, modsel(att_mod_w, att_mod_b), att_norm_g, att_w_in, att_q_norm_g,
                     att_k_norm_g, att_w_out)
    return _final_norm(xl, final_norm_g)
```

```python
import functools
import math

import numpy as np
import jax
import jax.numpy as jnp
from jax import lax
from jax.experimental import pallas as pl
from jax.experimental.pallas import tpu as pltpu

F32 = jnp.float32
BF16 = jnp.bfloat16
HIGHEST = lax.Precision.HIGHEST

NORM_EPS = 1e-6
ROPE_THETA = 10000.0
GRID_W = 64
LANES = 128
SUBLANES = 8
VMEM_LIMIT = 59 * 1024 * 1024

TOK_TILE = 256

LRU_C = 8.0
LRU_BLOCK = 128
LRU_T = 64
LRU_WC = 512

RET_HEADS = 4
RET_DK = 256
RET_DV = 512
RET_CHUNK = 256

HY_EMB = 33
HY_HIDDEN = 64
HY_FAST_DECAY = 0.3
HY_SLOW_DECAY = 1.5
HY_TARGET = 1e-2
HY_CB = 8
HY_CB_SHORT = 32
HY_SHORT_SEQ = 4096

ATT_HEADS = 8
ATT_KV = 2
ATT_D = 128
ATT_TQ = 128
ATT_CHUNK_MAX = 256


def _cparams(sem):
    return pltpu.CompilerParams(dimension_semantics=sem, vmem_limit_bytes=VMEM_LIMIT)


def _dot(a, b, precision=None):
    return lax.dot_general(a, b, (((1,), (0,)), ((), ())), precision=precision,
                           preferred_element_type=F32)


def _dot_nt(a, b, precision=None):
    return lax.dot_general(a, b, (((1,), (1,)), ((), ())), precision=precision,
                           preferred_element_type=F32)


def _dot_tn(a, b):
    return lax.dot_general(a, b, (((0,), (0,)), ((), ())), preferred_element_type=F32)


def _silu(x):
    xh = 0.5 * x
    return xh * jnp.tanh(xh) + xh


def _mod_kernel(c_ref, w_ref, b_ref, o_ref):
    o_ref[...] = _dot(_silu(c_ref[...]), w_ref[...], HIGHEST) + b_ref[...]


def _modulation(cvec, mod_w, mod_b):
    D = cvec.shape[1]
    return pl.pallas_call(
        _mod_kernel,
        grid=(3,),
        in_specs=[pl.BlockSpec((16, D), lambda n: (0, 0)),
                  pl.BlockSpec((D, D), lambda n: (0, n)),
                  pl.BlockSpec((1, D), lambda n: (0, n))],
        out_specs=pl.BlockSpec((16, D), lambda n: (0, n)),
        out_shape=jax.ShapeDtypeStruct((16, 3 * D), F32),
        compiler_params=_cparams(("arbitrary",)),
        name="modulation",
    )(cvec, mod_w, mod_b.reshape(1, 3 * D))


def _nmm_kernel(mode, x_ref, mod_ref, g_ref, w_ref, *rest):
    o_ref = rest[-1]
    x = x_ref[0]
    ms = jnp.mean(x * x, axis=-1, keepdims=True)
    xn = x * lax.rsqrt(ms + NORM_EPS) * g_ref[...]
    mod = mod_ref[0, 0]
    h = xn * (1.0 + mod[1:2]) + mod[0:1]
    u = _dot(h.astype(BF16), w_ref[...])
    if mode == "plain":
        o_ref[0] = u.astype(o_ref.dtype)
    elif mode == "ret":
        cos = rest[0][...]
        sin = rest[1][...]
        nk = RET_HEADS * RET_DK
        half = RET_DK // 2
        for base, scale in ((0, 1.0), (nk, RET_DK ** -0.5)):
            for hh in range(RET_HEADS):
                c0 = base + hh * RET_DK
                x1 = u[:, c0:c0 + half]
                x2 = u[:, c0 + half:c0 + RET_DK]
                o_ref[0, :, c0:c0 + half] = ((x1 * cos - x2 * sin) * scale).astype(o_ref.dtype)
                o_ref[0, :, c0 + half:c0 + RET_DK] = ((x1 * sin + x2 * cos) * scale).astype(o_ref.dtype)
        o_ref[0, :, 2 * nk:] = u[:, 2 * nk:].astype(o_ref.dtype)
    elif mode == "att":
        c2 = rest[0][...]
        s2 = rest[1][...]
        qg = rest[2][...]
        kg = rest[3][...]
        nqk = ATT_HEADS + ATT_KV
        for hh in range(nqk):
            xh = u[:, hh * ATT_D:(hh + 1) * ATT_D]
            g = qg if hh < ATT_HEADS else kg
            r = lax.rsqrt(jnp.mean(xh * xh, axis=-1, keepdims=True) + NORM_EPS)
            xh = xh * r * g
            rot = xh * c2 + pltpu.roll(xh, ATT_D // 2, 1) * s2
            if hh < ATT_HEADS:
                rot = rot * (ATT_D ** -0.5 * math.log2(math.e))
            o_ref[0, :, hh * ATT_D:(hh + 1) * ATT_D] = rot.astype(o_ref.dtype)
        o_ref[0, :, nqk * ATT_D:] = u[:, nqk * ATT_D:].astype(o_ref.dtype)
    else:
        raise ValueError(mode)


def _norm_proj(x, modsel, norm_g, w_bf16, n_ctx_blk, mode="plain", extras=()):
    B, LT, D = x.shape
    F = w_bf16.shape[1]
    T = TOK_TILE
    tab_specs = []
    for e in extras:
        if e.shape[0] == 1:
            tab_specs.append(pl.BlockSpec(e.shape, lambda b, i: (0, 0)))
        else:
            tab_specs.append(pl.BlockSpec((T, e.shape[1]), lambda b, i: (i, 0)))
    return pl.pallas_call(
        functools.partial(_nmm_kernel, mode),
        grid=(B, LT // T),
        in_specs=[pl.BlockSpec((1, T, D), lambda b, i: (b, i, 0)),
                  pl.BlockSpec((1, 1, 3, D), lambda b, i: (jnp.where(i < n_ctx_blk, 0, 1), b, 0, 0)),
                  pl.BlockSpec((1, D), lambda b, i: (0, 0)),
                  pl.BlockSpec((D, F), lambda b, i: (0, 0), pipeline_mode=pl.Buffered(1)),
                  ] + tab_specs,
        out_specs=pl.BlockSpec((1, T, F), lambda b, i: (b, i, 0)),
        out_shape=jax.ShapeDtypeStruct((B, LT, F), BF16),
        compiler_params=_cparams(("arbitrary", "arbitrary")),
        name="norm_proj_" + mode,
    )(x, modsel, norm_g.reshape(1, D), w_bf16, *extras)


def _opr_kernel(z_ref, w_ref, x_ref, mod_ref, *rest):
    o_ref = rest[-1]
    y = _dot(z_ref[0], w_ref[...])
    x = x_ref[0] + mod_ref[0, 0][2:3] * y
    if len(rest) == 2:
        ms = jnp.mean(x * x, axis=-1, keepdims=True)
        x = x * lax.rsqrt(ms + NORM_EPS) * rest[0][...]
    o_ref[0] = x


def _out_proj_residual(z, w_bf16, x, modsel, n_ctx_blk, x_off_blk=0, final_g=None):
    B, Lz, Fz = z.shape
    D = x.shape[2]
    T = TOK_TILE
    in_specs = [pl.BlockSpec((1, T, Fz), lambda b, i: (b, i, 0)),
                pl.BlockSpec((Fz, D), lambda b, i: (0, 0), pipeline_mode=pl.Buffered(1)),
                pl.BlockSpec((1, T, D), lambda b, i: (b, i + x_off_blk, 0)),
                pl.BlockSpec((1, 1, 3, D),
                             lambda b, i: (jnp.where(i + x_off_blk < n_ctx_blk, 0, 1), b, 0, 0))]
    args = [z, w_bf16, x, modsel]
    if final_g is not None:
        in_specs.append(pl.BlockSpec((1, D), lambda b, i: (0, 0)))
        args.append(final_g.astype(F32).reshape(1, D))
    return pl.pallas_call(
        _opr_kernel,
        grid=(B, Lz // T),
        in_specs=in_specs,
        out_specs=pl.BlockSpec((1, T, D), lambda b, i: (b, i, 0)),
        out_shape=jax.ShapeDtypeStruct((B, Lz, D), F32),
        compiler_params=_cparams(("arbitrary", "arbitrary")),
        name="out_proj_residual",
    )(*args)


def _lru_kernel(fwd, T, n_ctx, n_tot, u_ref, up_ref, un_ref, cw_ref, cb_ref, wg_ref, bg_ref,
                lam_ref, *rest):
    if fwd:
        gate_ref, latb_ref, o_ref, h_sc, ext_sc, hs_sc = rest
    else:
        o_ref, h_sc, ext_sc = rest
    Wc = u_ref.shape[2]
    NB = SUBLANES
    HALO = up_ref.shape[1]
    s = pl.program_id(1)
    if fwd:
        c = s
    else:
        c = jnp.where(s < n_ctx, n_ctx - 1 - s, n_tot + n_ctx - 1 - s)

    @pl.when(s == 0)
    def _():
        h_sc[...] = jnp.zeros_like(h_sc)

    first = jnp.logical_or(c == 0, c == n_ctx)
    last = jnp.logical_or(c == n_ctx - 1, c == n_tot - 1)
    pm = jnp.where(first, 0.0, 1.0)
    nm = jnp.where(last, 0.0, 1.0)
    for b in range(NB):
        prev = up_ref[b, HALO - 2:HALO, :].astype(F32) * pm
        nxt = un_ref[b, 0:1, :].astype(F32) * nm
        for n in range(Wc // LRU_BLOCK):
            sl = slice(n * LRU_BLOCK, (n + 1) * LRU_BLOCK)
            ext_sc[n, pl.ds(b, 1), :] = prev[0:1, sl]
            ext_sc[n, pl.ds(NB + b, 1), :] = prev[1:2, sl]
            ext_sc[n, pl.ds(2 * NB + b, T, stride=NB), :] = u_ref[b, :, sl].astype(F32)
            ext_sc[n, pl.ds((T + 2) * NB + b, 1), :] = nxt[:, sl]
    lam = lam_ref[...]
    sp = jnp.maximum(-lam, 0.0) + jnp.log1p(jnp.exp(-jnp.abs(lam)))
    c2 = (-0.5 * LRU_C * math.log2(math.e)) * sp
    for n in range(Wc // LRU_BLOCK):
        sl = slice(n * LRU_BLOCK, (n + 1) * LRU_BLOCK)
        xn = cb_ref[:, sl] + cw_ref[0:1, sl] * ext_sc[n, 0:T * NB, :]
        for k in range(1, 4):
            xn = xn + cw_ref[k:k + 1, sl] * ext_sc[n, k * NB:(k + T) * NB, :]
        g = _dot(xn.astype(BF16), wg_ref[n])
        tr = jnp.tanh(g[:, :LRU_BLOCK] + bg_ref[0:1, sl])
        ti = jnp.tanh(g[:, LRU_BLOCK:] + bg_ref[1:2, sl])
        a = jnp.exp2(c2[:, sl] * tr + c2[:, sl])
        y = 1.0 - a * a
        xh = 0.5 * xn
        bb = jnp.where(y > 0.0, y * lax.rsqrt(y), 0.0) * (xh * ti + xh)
        a3 = a.reshape(T, SUBLANES, LRU_BLOCK)
        b3 = bb.reshape(T, SUBLANES, LRU_BLOCK)
        h = h_sc[:, sl]
        for t in (range(T) if fwd else range(T - 1, -1, -1)):
            h = a3[t] * h + b3[t]
            if fwd:
                hs_sc[n, t * NB:(t + 1) * NB, :] = h + latb_ref[t, :, sl]
            else:
                o_ref[t, :, sl] = h
        h_sc[:, sl] = h
        if fwd:
            for b in range(NB):
                hb = hs_sc[n, pl.ds(b, T, stride=NB), :]
                o_ref[b, :, sl] = (hb * _silu(gate_ref[b, :, sl].astype(F32))).astype(o_ref.dtype)


def _lru_pass(fwd, u, conv_w, conv_b, wg, bg, lam, n_ctx_rows, latb=None):
    B, LT, W2 = u.shape
    W = W2 // 2
    T, Wc = LRU_T, LRU_WC
    n_tot = LT // T
    n_ctx = n_ctx_rows // T
    nj = W // Wc
    halo = 2 * SUBLANES
    hb = T // halo

    def chunk(s):
        if fwd:
            return s
        return jnp.where(s < n_ctx, n_ctx - 1 - s, n_tot + n_ctx - 1 - s)

    in_specs = [
        pl.BlockSpec((B, T, Wc), lambda j, s: (0, chunk(s), j)),
        pl.BlockSpec((B, halo, Wc), lambda j, s: (0, jnp.maximum(chunk(s) * hb - 1, 0), j)),
        pl.BlockSpec((B, halo, Wc),
                     lambda j, s: (0, jnp.minimum((chunk(s) + 1) * hb, LT // halo - 1), j)),
        pl.BlockSpec((4, Wc), lambda j, s: (0, j)),
        pl.BlockSpec((1, Wc), lambda j, s: (0, j)),
        pl.BlockSpec((Wc // LRU_BLOCK, LRU_BLOCK, 2 * LRU_BLOCK), lambda j, s: (j, 0, 0)),
        pl.BlockSpec((2, Wc), lambda j, s: (0, j)),
        pl.BlockSpec((1, Wc), lambda j, s: (0, j)),
    ]
    args = [u, u, u, conv_w, conv_b.reshape(1, W), wg, bg, lam.reshape(1, W)]
    nb = Wc // LRU_BLOCK
    scratch = [pltpu.VMEM((B, Wc), F32), pltpu.VMEM((nb, (T + 3) * B, LRU_BLOCK), F32)]
    if fwd:
        in_specs += [pl.BlockSpec((B, T, Wc), lambda j, s: (0, s, nj + j)),
                     pl.BlockSpec((T, B, Wc), lambda j, s: (s, 0, j))]
        args += [u, latb]
        scratch.append(pltpu.VMEM((nb, T * B, LRU_BLOCK), F32))
        out_specs = pl.BlockSpec((B, T, Wc), lambda j, s: (0, s, j))
        out_shape = jax.ShapeDtypeStruct((B, LT, W), BF16)
    else:
        out_specs = pl.BlockSpec((T, B, Wc), lambda j, s: (chunk(s), 0, j))
        out_shape = jax.ShapeDtypeStruct((LT, B, W), F32)
    return pl.pallas_call(
        functools.partial(_lru_kernel, fwd, T, n_ctx, n_tot),
        grid=(nj, n_tot),
        in_specs=in_specs,
        out_specs=out_specs,
        out_shape=out_shape,
        scratch_shapes=scratch,
        compiler_params=_cparams(("arbitrary", "arbitrary")),
        name="lru_fwd" if fwd else "lru_bwd",
    )(*args)


def _ret_kernel(fwd, C, lg_ref, blk_ref, q_ref, k_ref, v_ref, *rest):
    if fwd:
        gate_ref, ob_ref, o_ref, st_sc, qdec_sc, kdec_sc, mask_sc = rest
    else:
        o_ref, st_sc, qdec_sc, kdec_sc = rest
    s = pl.program_id(1)
    d = 0 if fwd else 1

    @pl.when(s == 0)
    def _():
        st_sc[...] = jnp.zeros_like(st_sc)

    @pl.when(jnp.logical_and(pl.program_id(0) == 0, s == 0))
    def _():
        row_v = lax.broadcasted_iota(jnp.int32, (C, RET_DV), 0).astype(F32)
        row_k = lax.broadcasted_iota(jnp.int32, (C, RET_DK), 0).astype(F32)
        if fwd:
            ri = lax.broadcasted_iota(jnp.int32, (C, C), 0)
            ci = lax.broadcasted_iota(jnp.int32, (C, C), 1)
            diff = (ri - ci).astype(F32)
        for hh in range(RET_HEADS):
            lg = lg_ref[d, hh]
            if fwd:
                qdec_sc[hh] = jnp.exp((row_v + 1.0) * lg)
                kdec_sc[hh] = jnp.exp((C - 1.0 - row_k) * lg)
                mask_sc[hh] = jnp.exp(jnp.where(diff >= 0, diff * lg_ref[0, hh],
                                                -diff * lg_ref[1, hh]))
            else:
                qdec_sc[hh] = jnp.exp((C - row_v) * lg)
                kdec_sc[hh] = jnp.exp(row_k * lg)

    for hh in range(RET_HEADS):
        qh = q_ref[0, :, hh * RET_DK:(hh + 1) * RET_DK]
        kh = k_ref[0, :, hh * RET_DK:(hh + 1) * RET_DK]
        vh = v_ref[0, :, hh * RET_DV:(hh + 1) * RET_DV]
        st = st_sc[hh]
        o = _dot(qh, st.astype(BF16)) * qdec_sc[hh]
        kd = (kh.astype(F32) * kdec_sc[hh]).astype(BF16)
        st_sc[hh] = st * blk_ref[d, hh] + _dot_tn(kd, vh)
        if fwd:
            sc = _dot_nt(qh, kh) * mask_sc[hh]
            o = o + _dot(sc.astype(BF16), vh)
            o = o + ob_ref[0, :, hh * RET_DV:(hh + 1) * RET_DV].astype(F32)
            o = o * lax.rsqrt(jnp.mean(o * o, axis=-1, keepdims=True) + NORM_EPS)
            gate = gate_ref[0, :, hh * RET_DV:(hh + 1) * RET_DV].astype(F32)
            o_ref[0, :, hh * RET_DV:(hh + 1) * RET_DV] = (_silu(gate) * o).astype(o_ref.dtype)
        else:
            o_ref[0, :, hh * RET_DV:(hh + 1) * RET_DV] = o.astype(o_ref.dtype)


def _ret_pass(fwd, u, logg, blk, n_ctx_rows, ob=None):
    B, LT, _ = u.shape
    C = RET_CHUNK
    nk = RET_HEADS * RET_DK
    nv = RET_HEADS * RET_DV
    n_tot = LT // C
    n_ctx = n_ctx_rows // C

    def chunk(s):
        if fwd:
            return s
        return jnp.where(s < n_ctx, n_ctx - 1 - s, n_tot + n_ctx - 1 - s)

    smem = pl.BlockSpec(memory_space=pltpu.SMEM)
    in_specs = [smem, smem,
                pl.BlockSpec((1, C, nk), lambda b, s: (b, chunk(s), 0)),
                pl.BlockSpec((1, C, nk), lambda b, s: (b, chunk(s), 1)),
                pl.BlockSpec((1, C, nv), lambda b, s: (b, chunk(s), 1))]
    args = [logg, blk, u, u, u]
    if fwd:
        in_specs += [pl.BlockSpec((1, C, nv), lambda b, s: (b, s, 2)),
                     pl.BlockSpec((1, C, nv), lambda b, s: (b, s, 0))]
        args += [u, ob]
    return pl.pallas_call(
        functools.partial(_ret_kernel, fwd, C),
        grid=(B, n_tot),
        in_specs=in_specs,
        out_specs=pl.BlockSpec((1, C, nv), lambda b, s: (b, chunk(s), 0)),
        out_shape=jax.ShapeDtypeStruct((B, LT, nv), BF16),
        scratch_shapes=[pltpu.VMEM((RET_HEADS, RET_DK, RET_DV), F32),
                        pltpu.VMEM((RET_HEADS, C, RET_DV), F32),
                        pltpu.VMEM((RET_HEADS, C, RET_DK), F32)]
        + ([pltpu.VMEM((RET_HEADS, C, C), F32)] if fwd else []),
        compiler_params=_cparams(("arbitrary", "arbitrary")),
        name="ret_fwd" if fwd else "ret_bwd",
    )(*args)


def _hy_filter_kernel(z_ref, tt_ref, fw1_ref, fb1_ref, fw2_ref, fb2_ref, fq_ref, w3_ref,
                      delta_ref, o_ref):
    fq = fq_ref[...]
    h1 = jnp.sin(fq * (_dot(z_ref[...], fw1_ref[...], HIGHEST) + fb1_ref[...]))
    h2 = jnp.sin(fq * (_dot(h1, fw2_ref[...], HIGHEST) + fb2_ref[...]))
    kt = _dot_nt(w3_ref[0], h2, HIGHEST)
    t = tt_ref[0, 0:1, :]
    msk = tt_ref[0, 1:2, :]
    o_ref[0] = kt * jnp.exp(-delta_ref[...] * t) * msk


def _hy_positions(Lv, R):
    N = R * LANES
    p = np.arange(N)
    fwd = p < Lv
    bwd = p > N - Lv
    tap = np.where(fwd, p, np.where(bwd, N - p, 0))
    return tap.astype(np.int32), (fwd | bwd).astype(np.float32)


def _hy_filter_taps(Lv, R, fw1, fb1, fw2, fb2, fw3, freq):
    W = fw3.shape[1] // 4
    bands = (HY_EMB - 1) // 2
    t = jnp.linspace(0.0, 1.0, Lv, dtype=F32)[:, None]
    w = 2.0 * math.pi * jnp.arange(Lv, dtype=F32)[:, None] / Lv
    fr = jnp.linspace(1e-4, bands - 1, bands, dtype=F32)[None, :]
    z = jnp.concatenate([t, jnp.cos(fr * w), -jnp.sin(fr * w)], axis=-1)
    tap, valid = _hy_positions(Lv, R)
    zp = jnp.pad(z[tap], ((0, 0), (0, LANES - HY_EMB)))
    tt = jnp.stack([t[tap, 0].reshape(R, LANES), jnp.asarray(valid).reshape(R, LANES)], axis=1)
    fw1p = jnp.pad(fw1.astype(F32), ((0, LANES - HY_EMB), (0, 0)))
    w3 = fw3.astype(F32).reshape(HY_HIDDEN, 2, 2, W)
    w3d = jnp.stack([w3[:, :, 0, :].reshape(HY_HIDDEN, 2 * W).T,
                     w3[:, :, 1, :].reshape(HY_HIDDEN, 2 * W).T], axis=0)
    deltas = jnp.abs(jnp.linspace(math.log(HY_TARGET) / HY_SLOW_DECAY,
                                  math.log(HY_TARGET) / HY_FAST_DECAY, W, dtype=F32))
    delta2 = jnp.concatenate([deltas, deltas]).reshape(2 * W, 1)
    F2 = 2 * W
    half = R // 2
    kt = pl.pallas_call(
        _hy_filter_kernel,
        grid=(R,),
        in_specs=[pl.BlockSpec((LANES, LANES), lambda r: (r, 0)),
                  pl.BlockSpec((1, 2, LANES), lambda r: (r, 0, 0)),
                  pl.BlockSpec((LANES, HY_HIDDEN), lambda r: (0, 0)),
                  pl.BlockSpec((1, HY_HIDDEN), lambda r: (0, 0)),
                  pl.BlockSpec((HY_HIDDEN, HY_HIDDEN), lambda r: (0, 0)),
                  pl.BlockSpec((1, HY_HIDDEN), lambda r: (0, 0)),
                  pl.BlockSpec((1, HY_HIDDEN), lambda r: (0, 0)),
                  pl.BlockSpec((1, F2, HY_HIDDEN), lambda r: (jnp.where(r < half, 0, 1), 0, 0)),
                  pl.BlockSpec((F2, 1), lambda r: (0, 0))],
        out_specs=pl.BlockSpec((1, F2, LANES), lambda r: (r, 0, 0)),
        out_shape=jax.ShapeDtypeStruct((R, F2, LANES), F32),
        compiler_params=_cparams(("arbitrary",)),
        name="hyena_filter_taps",
    )(zp, tt, fw1p, fb1.astype(F32).reshape(1, -1), fw2.astype(F32), fb2.astype(F32).reshape(1, -1),
      freq.astype(F32).reshape(1, -1), w3d, delta2)
    return jnp.transpose(kt, (1, 0, 2))


def _dft_constants(R0, R):
    N = R * LANES
    k1 = np.arange(R)[:, None]
    n1 = np.arange(R0)[None, :]
    f1 = np.exp(-2j * np.pi * k1 * n1 / R)
    ma = np.block([[f1.real, -f1.imag], [f1.imag, f1.real]])
    g1 = np.exp(2j * np.pi * n1.T * k1.T / R) / N
    mai = np.block([[g1.real, -g1.imag], [g1.imag, g1.real]])
    n2 = np.arange(LANES)
    tw = np.exp(-2j * np.pi * np.arange(R)[:, None] * n2[None, :] / N)
    f2 = np.exp(-2j * np.pi * n2[:, None] * n2[None, :] / LANES)
    mc = np.block([[f2.real, f2.imag], [-f2.imag, f2.real]])
    mci = np.block([[f2.real, -f2.imag], [f2.imag, f2.real]])
    f1_full = np.exp(-2j * np.pi * k1 * np.arange(R)[None, :] / R)
    as_bf = lambda m: jnp.asarray(m, F32).astype(BF16)
    return dict(ma=as_bf(ma), mai=as_bf(mai), mc=as_bf(mc), mci=as_bf(mci),
                tr=jnp.asarray(tw.real, F32), ti=jnp.asarray(tw.imag, F32),
                ma_real=as_bf(np.concatenate([f1_full.real, f1_full.imag], axis=0)))


def _hy_fft_kernel(R, cb, k_ref, ma_ref, tr_ref, ti_ref, mc_ref, or_ref, oi_ref):
    cols = [k_ref[c] for c in range(cb)]
    inv = []
    for col in cols:
        sabs = jnp.sum(jnp.sum(jnp.abs(col), axis=1, keepdims=True), axis=0, keepdims=True)
        inv.append(1.0 / sabs)
    rhs = jnp.concatenate(cols, axis=1)
    hi = rhs.astype(BF16)
    lo = (rhs - hi.astype(F32)).astype(BF16)
    a = _dot(ma_ref[...], hi) + _dot(ma_ref[...], lo)
    tr = tr_ref[...]
    ti = ti_ref[...]
    rows = []
    for c in range(cb):
        ar = a[:R, c * LANES:(c + 1) * LANES]
        ai = a[R:, c * LANES:(c + 1) * LANES]
        rows.append(jnp.concatenate([ar * tr - ai * ti, ar * ti + ai * tr], axis=1))
    lhs = jnp.concatenate(rows, axis=0)
    hi = lhs.astype(BF16)
    lo = (lhs - hi.astype(F32)).astype(BF16)
    x = _dot(hi, mc_ref[...]) + _dot(lo, mc_ref[...])
    for c in range(cb):
        or_ref[c] = x[c * R:(c + 1) * R, :LANES] * inv[c]
        oi_ref[c] = x[c * R:(c + 1) * R, LANES:] * inv[c]


def _hy_filter_spectrum(kt, R, consts):
    F2 = kt.shape[0]
    cb = HY_CB_SHORT
    spec = pl.BlockSpec((cb, R, LANES), lambda j: (j, 0, 0))
    full = lambda a: pl.BlockSpec(a.shape, lambda j: (0,) * a.ndim)
    return pl.pallas_call(
        functools.partial(_hy_fft_kernel, R, cb),
        grid=(F2 // cb,),
        in_specs=[spec, full(consts["ma_real"]), full(consts["tr"]), full(consts["ti"]),
                  full(consts["mc"])],
        out_specs=[spec, spec],
        out_shape=[jax.ShapeDtypeStruct((F2, R, LANES), F32)] * 2,
        compiler_params=_cparams(("arbitrary",)),
        name="hyena_filter_fft",
    )(kt, consts["ma_real"], consts["tr"], consts["ti"], consts["mc"])


def _hy_conv_kernel(R0, R, Lv, cb, W, cw_ref, cbias_ref, skip_ref, v_ref, x1_ref, x2_ref, g_ref,
                    k0r_ref, k0i_ref, k1r_ref, k1i_ref, ma_ref, mai_ref, mc_ref, mci_ref,
                    tr_ref, ti_ref, o_ref):
    j = pl.program_id(0)
    row = lax.broadcasted_iota(jnp.int32, (R0, LANES), 0)
    lane = lax.broadcasted_iota(jnp.int32, (R0, LANES), 1)
    pos = row * LANES + lane
    tr = tr_ref[...]
    ti = ti_ref[...]

    def short_conv(ref, part, c, grp):
        x = ref[0, part, c].astype(F32)
        ch = grp * W + j * cb + c
        r = pltpu.roll(x, 1, 1)
        prev = jnp.where(lane == 0, pltpu.roll(r, 1, 0), r)
        prev = jnp.where(pos == 0, 0.0, prev)
        r = pltpu.roll(x, LANES - 1, 1)
        nxt = jnp.where(lane == LANES - 1, pltpu.roll(r, R0 - 1, 0), r)
        nxt = jnp.where(pos == R0 * LANES - 1, 0.0, nxt)
        y = cw_ref[0, ch] * prev + cw_ref[1, ch] * x + cw_ref[2, ch] * nxt + cbias_ref[ch]
        if Lv < R0 * LANES:
            y = jnp.where(pos < Lv, y, 0.0)
        return y

    def long_conv(re, im, kr_ref, ki_ref):
        rhs = jnp.concatenate([jnp.concatenate([re[c], im[c]], axis=0) for c in range(cb)], axis=1)
        a = _dot(ma_ref[...], rhs.astype(BF16))
        rows = []
        for c in range(cb):
            ar = a[:R, c * LANES:(c + 1) * LANES]
            ai = a[R:, c * LANES:(c + 1) * LANES]
            rows.append(jnp.concatenate([ar * tr - ai * ti, ar * ti + ai * tr], axis=1))
        x = _dot(jnp.concatenate(rows, axis=0).astype(BF16), mc_ref[...])
        rows = []
        for c in range(cb):
            xr = x[c * R:(c + 1) * R, :LANES]
            xi = x[c * R:(c + 1) * R, LANES:]
            kr = kr_ref[c]
            ki = ki_ref[c]
            rows.append(jnp.concatenate([xr * kr - xi * ki, xr * ki + xi * kr], axis=1))
        b = _dot(jnp.concatenate(rows, axis=0).astype(BF16), mci_ref[...])
        cols = []
        for c in range(cb):
            br = b[c * R:(c + 1) * R, :LANES]
            bi = b[c * R:(c + 1) * R, LANES:]
            cols.append(jnp.concatenate([br * tr + bi * ti, bi * tr - br * ti], axis=0))
        y = _dot(mai_ref[...], jnp.concatenate(cols, axis=1).astype(BF16))
        out_re = [y[:R0, c * LANES:(c + 1) * LANES] for c in range(cb)]
        out_im = [y[R0:, c * LANES:(c + 1) * LANES] for c in range(cb)]
        return out_re, out_im

    v = [[short_conv(v_ref, p, c, 0) for c in range(cb)] for p in range(2)]
    x1 = [[short_conv(x1_ref, p, c, 1) for c in range(cb)] for p in range(2)]
    cr, ci = long_conv(v[0], v[1], k0r_ref, k0i_ref)
    conv = (cr, ci)
    y = [[x1[p][c] * (conv[p][c] + v[p][c] * skip_ref[0, j * cb + c]) for c in range(cb)]
         for p in range(2)]
    x2 = [[short_conv(x2_ref, p, c, 2) for c in range(cb)] for p in range(2)]
    cr, ci = long_conv(y[0], y[1], k1r_ref, k1i_ref)
    conv = (cr, ci)
    for p in range(2):
        for c in range(cb):
            y2 = x2[p][c] * (conv[p][c] + y[p][c] * skip_ref[1, j * cb + c])
            gate = g_ref[0, p, c].astype(F32)
            o_ref[0, p, c] = (y2 * _silu(gate)).astype(o_ref.dtype)


def _hy_conv(ut, kfr, kfi, conv_w, conv_b, skip, Lv, R0, R, consts):
    G, _, W4, _, _ = ut.shape
    W = W4 // 4
    cb = HY_CB if R0 * LANES >= HY_SHORT_SEQ else HY_CB_SHORT
    nj = W // cb
    smem = pl.BlockSpec(memory_space=pltpu.SMEM)
    full = lambda a: pl.BlockSpec(a.shape, lambda j, g: (0,) * a.ndim)

    def data(grp):
        return pl.BlockSpec((1, 2, cb, R0, LANES), lambda j, g: (g, 0, grp * nj + j, 0, 0))

    def filt(order):
        return pl.BlockSpec((cb, R, LANES), lambda j, g: (order * nj + j, 0, 0))

    return pl.pallas_call(
        functools.partial(_hy_conv_kernel, R0, R, Lv, cb, W),
        grid=(nj, G),
        in_specs=[smem, smem, smem, data(0), data(1), data(2), data(3),
                  filt(0), filt(0), filt(1), filt(1),
                  full(consts["ma"]), full(consts["mai"]), full(consts["mc"]), full(consts["mci"]),
                  full(consts["tr"]), full(consts["ti"])],
        out_specs=pl.BlockSpec((1, 2, cb, R0, LANES), lambda j, g: (g, 0, j, 0, 0)),
        out_shape=jax.ShapeDtypeStruct((G, 2, W, R0, LANES), BF16),
        compiler_params=_cparams(("arbitrary", "arbitrary")),
        name="hyena_conv",
    )(conv_w, conv_b, skip, ut, ut, ut, ut, kfr, kfi, kfr, kfi,
      consts["ma"], consts["mai"], consts["mc"], consts["mci"], consts["tr"], consts["ti"])


def _hy_segment(u_seg, conv_w, conv_b, fw1, fb1, fw2, fb2, fw3, freq, skip):
    B, rows, _, W4 = u_seg.shape
    W = W4 // 4
    Lv = rows * LANES
    R0 = -(-rows // SUBLANES) * SUBLANES
    R = max(R0, -(-(2 * rows) // SUBLANES) * SUBLANES)
    consts = _dft_constants(R0, R)
    kt = _hy_filter_taps(Lv, R, fw1, fb1, fw2, fb2, fw3, freq)
    kfr, kfi = _hy_filter_spectrum(kt, R, consts)
    ut = jnp.transpose(u_seg, (0, 3, 1, 2))
    if R0 > rows:
        ut = jnp.pad(ut, ((0, 0), (0, 0), (0, R0 - rows), (0, 0)))
    ut = ut.reshape(B // 2, 2, W4, R0, LANES)
    z = _hy_conv(ut, kfr, kfi, conv_w.astype(F32), conv_b.astype(F32), skip.astype(F32),
                 Lv, R0, R, consts)
    z = z.reshape(B, W, R0, LANES)[:, :, :rows]
    return jnp.transpose(z, (0, 2, 3, 1)).reshape(B, Lv, W)


def _attn_kernel(n_chunks, ck, q0_ref, q1_ref, q2_ref, k_ref, v_ref, g_ref, o_ref,
                 sa_sc, sb_sc, ma_sc):
    R = ATT_HEADS // ATT_KV
    tq = q1_ref.shape[1]

    def stack_heads(q_ref):
        return jnp.concatenate([q_ref[0, :, r * ATT_D:(r + 1) * ATT_D] for r in range(R)], axis=0)

    def scores(q, s_sc, c):
        s = _dot_nt(q, k_ref[0, c * ck:(c + 1) * ck, :])
        s_sc[:, c * ck:(c + 1) * ck] = s
        return jnp.max(s, axis=-1, keepdims=True)

    def weighted(s_sc, m, c):
        p = jnp.exp2(s_sc[:, c * ck:(c + 1) * ck] - m)
        return (jnp.sum(p, axis=-1, keepdims=True),
                _dot(p.astype(BF16), v_ref[0, c * ck:(c + 1) * ck, :]))

    def finish(acc, l, half):
        o = acc / l
        for r in range(R):
            rows = slice(half * tq, (half + 1) * tq)
            gate = g_ref[0, rows, r * ATT_D:(r + 1) * ATT_D].astype(F32)
            o_ref[0, rows, r * ATT_D:(r + 1) * ATT_D] = (
                _silu(gate) * o[r * tq:(r + 1) * tq]).astype(o_ref.dtype)

    def phase(s_cur, m_cur, q_next, s_next):
        m_next = l = acc = None
        for c in range(n_chunks):
            mc = scores(q_next, s_next, c)
            lc, pv = weighted(s_cur, m_cur, c)
            m_next = mc if m_next is None else jnp.maximum(m_next, mc)
            l = lc if l is None else l + lc
            acc = pv if acc is None else acc + pv
        return acc, l, m_next

    @pl.when(pl.program_id(2) == 0)
    def _():
        q0 = stack_heads(q0_ref)
        m = None
        for c in range(n_chunks):
            mc = scores(q0, sa_sc, c)
            m = mc if m is None else jnp.maximum(m, mc)
        ma_sc[...] = m

    acc, l, mb = phase(sa_sc, ma_sc[...], stack_heads(q1_ref), sb_sc)
    finish(acc, l, 0)
    acc, l, ma = phase(sb_sc, mb, stack_heads(q2_ref), sa_sc)
    finish(acc, l, 1)
    ma_sc[...] = ma


def _attention(u, n_ctx_rows):
    B, LT, _ = u.shape
    L = LT - n_ctx_rows
    R = ATT_HEADS // ATT_KV
    tq = ATT_TQ
    ck = max(t for t in range(LANES, ATT_CHUNK_MAX + 1, LANES) if LT % t == 0)
    qw = R * ATT_D
    n_tiles = L // tq
    assert n_tiles % 2 == 0 and n_ctx_rows % (2 * tq) == 0
    q_off = n_ctx_rows // tq
    k_col = ATT_HEADS
    v_col = ATT_HEADS + ATT_KV
    g_col = (ATT_HEADS + 2 * ATT_KV) * ATT_D // qw

    def q_spec(tile):
        return pl.BlockSpec((1, tq, qw), lambda b, g, i: (b, q_off + tile(i), g))

    return pl.pallas_call(
        functools.partial(_attn_kernel, LT // ck, ck),
        grid=(B, ATT_KV, n_tiles // 2),
        in_specs=[q_spec(lambda i: 0),
                  q_spec(lambda i: 2 * i + 1),
                  q_spec(lambda i: jnp.minimum(2 * i + 2, n_tiles - 1)),
                  pl.BlockSpec((1, LT, ATT_D), lambda b, g, i: (b, 0, k_col + g),
                               pipeline_mode=pl.Buffered(1)),
                  pl.BlockSpec((1, LT, ATT_D), lambda b, g, i: (b, 0, v_col + g),
                               pipeline_mode=pl.Buffered(1)),
                  pl.BlockSpec((1, 2 * tq, qw), lambda b, g, i: (b, i + q_off // 2, g_col + g))],
        out_specs=pl.BlockSpec((1, 2 * tq, qw), lambda b, g, i: (b, i, g)),
        out_shape=jax.ShapeDtypeStruct((B, L, ATT_HEADS * ATT_D), BF16),
        scratch_shapes=[pltpu.VMEM((R * tq, LT), F32), pltpu.VMEM((R * tq, LT), F32),
                        pltpu.VMEM((R * tq, 1), F32)],
        compiler_params=_cparams(("arbitrary", "arbitrary", "arbitrary")),
        name="attention",
    )(u, u, u, u, u, u)


def _rope_angles(pos, dim):
    half = dim // 2
    inv = ROPE_THETA ** (-jnp.arange(half, dtype=F32) / half)
    return pos[:, None] * inv[None, :]


def _with_ctx_identity(cos_like, sin_like, n_ctx_rows):
    ones = jnp.ones((n_ctx_rows, cos_like.shape[1]), F32)
    return (jnp.concatenate([ones, cos_like], axis=0),
            jnp.concatenate([jnp.zeros_like(ones), sin_like], axis=0))


def _layer_lru(xs, Lc, ms, norm_g, w_in, conv_w, conv_b, w_r, b_r, w_i, b_i, lam, w_out):
    n_ctx_blk = Lc // TOK_TILE
    u = _norm_proj(xs, ms, norm_g, w_in.astype(BF16), n_ctx_blk)
    lat = None
    for d in (1, 0):
        wg = (0.5 * jnp.concatenate([w_r[d], w_i[d]], axis=-1)).astype(BF16)
        bg = 0.5 * jnp.stack([b_r[d], b_i[d]], axis=0).astype(F32)
        lat = _lru_pass(d == 0, u, conv_w.astype(F32), conv_b.astype(F32), wg, bg,
                        lam[d].astype(F32), Lc, latb=lat)
    return _out_proj_residual(lat, w_out.astype(BF16), xs, ms, n_ctx_blk)


def _layer_ret(xs, Lc, ms, norm_g, w_in, decay_logit, w_out):
    n_ctx_blk = Lc // TOK_TILE
    L = xs.shape[1] - Lc
    ang = _rope_angles(jnp.arange(L, dtype=F32), RET_DK)
    cos, sin = _with_ctx_identity(jnp.cos(ang), jnp.sin(ang), Lc)
    u = _norm_proj(xs, ms, norm_g, w_in.astype(BF16), n_ctx_blk, "ret", (cos, sin))
    dl = decay_logit.astype(F32)
    logg = -(jnp.maximum(-dl, 0.0) + jnp.log1p(jnp.exp(-jnp.abs(dl))))
    blk = jnp.exp(RET_CHUNK * logg)
    ob = _ret_pass(False, u, logg, blk, Lc)
    z = _ret_pass(True, u, logg, blk, Lc, ob=ob)
    return _out_proj_residual(z, w_out.astype(BF16), xs, ms, n_ctx_blk)


def _layer_hyena(xs, Lc, ms, norm_g, w_in, conv_w, conv_b, fw1, fb1, fw2, fb2, fw3, freq, skip,
                 w_out):
    n_ctx_blk = Lc // TOK_TILE
    L = xs.shape[1] - Lc
    u = _norm_proj(xs, ms, norm_g, w_in.astype(BF16), n_ctx_blk)
    hy_args = (conv_w, conv_b, fw1, fb1, fw2, fb2, fw3, freq, skip)
    assert Lc % LANES == 0 and L % LANES == 0
    u4 = u.reshape(u.shape[0], (Lc + L) // LANES, LANES, u.shape[2])
    z = jnp.concatenate([_hy_segment(u4[:, :Lc // LANES], *hy_args),
                         _hy_segment(u4[:, Lc // LANES:], *hy_args)], axis=1)
    return _out_proj_residual(z, w_out.astype(BF16), xs, ms, n_ctx_blk)


def _layer_attn(xs, Lc, ms, norm_g, w_in, q_norm_g, k_norm_g, w_out, final_g=None):
    n_ctx_blk = Lc // TOK_TILE
    L = xs.shape[1] - Lc
    rows = L // GRID_W
    row = jnp.repeat(jnp.arange(rows, dtype=F32), GRID_W)
    col = jnp.tile(jnp.arange(GRID_W, dtype=F32), rows)
    ang = jnp.concatenate([_rope_angles(row, ATT_D // 2), _rope_angles(col, ATT_D // 2)], axis=-1)
    cos, sin = _with_ctx_identity(jnp.cos(ang), jnp.sin(ang), Lc)
    c2 = jnp.concatenate([cos, cos], axis=-1)
    s2 = jnp.concatenate([-sin, sin], axis=-1)
    u = _norm_proj(xs, ms, norm_g, w_in.astype(BF16), n_ctx_blk, "att",
                   (c2, s2, q_norm_g.astype(F32).reshape(1, ATT_D),
                    k_norm_g.astype(F32).reshape(1, ATT_D)))
    z = _attention(u, Lc)
    return _out_proj_residual(z, w_out.astype(BF16), xs, ms, n_ctx_blk, x_off_blk=n_ctx_blk,
                              final_g=final_g)


def kernel(x, c, ctx, c_ctx, lru_mod_w, lru_mod_b, lru_norm_g, lru_w_in, lru_conv_w, lru_conv_b, lru_w_r, lru_b_r, lru_w_i, lru_b_i, lru_lambda, lru_w_out, ret_mod_w, ret_mod_b, ret_norm_g, ret_w_in, ret_decay_logit, ret_w_out, hy_mod_w, hy_mod_b, hy_norm_g, hy_w_in, hy_conv_w, hy_conv_b, hy_fw1, hy_fb1, hy_fw2, hy_fb2, hy_fw3, hy_freq, hy_skip, hy_w_out, att_mod_w, att_mod_b, att_norm_g, att_w_in, att_q_norm_g, att_k_norm_g, att_w_out, final_norm_g):
    B, L, D = x.shape
    Lc = ctx.shape[1]
    LT = Lc + L
    assert B == SUBLANES and B % 2 == 0
    assert Lc % TOK_TILE == 0 and L % TOK_TILE == 0 and Lc % RET_CHUNK == 0 and Lc % LRU_T == 0
    n_ctx_blk = Lc // TOK_TILE

    cvec = jnp.zeros((16, D), F32).at[:B].set(c).at[B].set(c_ctx)

    def modsel(mod_w, mod_b):
        m = _modulation(cvec, mod_w, mod_b).reshape(16, 3, D)
        return jnp.stack([jnp.broadcast_to(m[B][None], (B, 3, D)), m[:B]], axis=0)

    xs = jnp.concatenate([ctx, x], axis=1)
    xs = _layer_lru(xs, Lc, modsel(lru_mod_w, lru_mod_b), lru_norm_g, lru_w_in, lru_conv_w,
                    lru_conv_b, lru_w_r, lru_b_r, lru_w_i, lru_b_i, lru_lambda, lru_w_out)
    xs = _layer_ret(xs, Lc, modsel(ret_mod_w, ret_mod_b), ret_norm_g, ret_w_in, ret_decay_logit,
                    ret_w_out)
    xs = _layer_hyena(xs, Lc, modsel(hy_mod_w, hy_mod_b), hy_norm_g, hy_w_in, hy_conv_w, hy_conv_b,
                      hy_fw1, hy_fb1, hy_fw2, hy_fb2, hy_fw3, hy_freq, hy_skip, hy_w_out)
    return _layer_attn(xs, Lc, modsel(att_mod_w, att_mod_b), att_norm_g, att_w_in, att_q_norm_g,
                       att_k_norm_g, att_w_out, final_g=final_norm_g)
```

```python
import functools
import math

import numpy as np
import jax
import jax.numpy as jnp
from jax import lax
from jax.experimental import pallas as pl
from jax.experimental.pallas import tpu as pltpu

F32 = jnp.float32
BF16 = jnp.bfloat16
HIGHEST = lax.Precision.HIGHEST

NORM_EPS = 1e-6
ROPE_THETA = 10000.0
GRID_W = 64
LANES = 128
SUBLANES = 8
VMEM_LIMIT = 59 * 1024 * 1024

TOK_TILE = 256

LRU_C = 8.0
LRU_BLOCK = 128
LRU_T = 128
LRU_WC = 512

RET_HEADS = 4
RET_DK = 256
RET_DV = 512
RET_CHUNK = 256

HY_EMB = 33
HY_HIDDEN = 64
HY_FAST_DECAY = 0.3
HY_SLOW_DECAY = 1.5
HY_TARGET = 1e-2
HY_CB = 8
HY_CB_SHORT = 32
HY_SHORT_SEQ = 4096

ATT_HEADS = 8
ATT_KV = 2
ATT_D = 128
ATT_TQ = 128
ATT_CHUNK_MAX = 256


def _cparams(sem):
    return pltpu.CompilerParams(dimension_semantics=sem, vmem_limit_bytes=VMEM_LIMIT)


def _dot(a, b, precision=None):
    return lax.dot_general(a, b, (((1,), (0,)), ((), ())), precision=precision,
                           preferred_element_type=F32)


def _dot_nt(a, b, precision=None):
    return lax.dot_general(a, b, (((1,), (1,)), ((), ())), precision=precision,
                           preferred_element_type=F32)


def _dot_tn(a, b):
    return lax.dot_general(a, b, (((0,), (0,)), ((), ())), preferred_element_type=F32)


def _silu(x):
    xh = 0.5 * x
    return xh * jnp.tanh(xh) + xh


def _mod_kernel(c_ref, w_ref, b_ref, o_ref):
    o_ref[...] = _dot(_silu(c_ref[...]), w_ref[...], HIGHEST) + b_ref[...]


def _modulation(cvec, mod_w, mod_b):
    D = cvec.shape[1]
    return pl.pallas_call(
        _mod_kernel,
        grid=(3,),
        in_specs=[pl.BlockSpec((16, D), lambda n: (0, 0)),
                  pl.BlockSpec((D, D), lambda n: (0, n)),
                  pl.BlockSpec((1, D), lambda n: (0, n))],
        out_specs=pl.BlockSpec((16, D), lambda n: (0, n)),
        out_shape=jax.ShapeDtypeStruct((16, 3 * D), F32),
        compiler_params=_cparams(("arbitrary",)),
        name="modulation",
    )(cvec, mod_w, mod_b.reshape(1, 3 * D))


def _nmm_kernel(mode, x_ref, mod_ref, g_ref, w_ref, *rest):
    o_ref = rest[-1]
    x = x_ref[0]
    ms = jnp.mean(x * x, axis=-1, keepdims=True)
    xn = x * lax.rsqrt(ms + NORM_EPS) * g_ref[...]
    mod = mod_ref[0, 0]
    h = xn * (1.0 + mod[1:2]) + mod[0:1]
    u = _dot(h.astype(BF16), w_ref[...])
    if mode == "plain":
        o_ref[0] = u.astype(o_ref.dtype)
    elif mode == "ret":
        cos = rest[0][...]
        sin = rest[1][...]
        nk = RET_HEADS * RET_DK
        half = RET_DK // 2
        for base, scale in ((0, 1.0), (nk, RET_DK ** -0.5)):
            for hh in range(RET_HEADS):
                c0 = base + hh * RET_DK
                x1 = u[:, c0:c0 + half]
                x2 = u[:, c0 + half:c0 + RET_DK]
                o_ref[0, :, c0:c0 + half] = ((x1 * cos - x2 * sin) * scale).astype(o_ref.dtype)
                o_ref[0, :, c0 + half:c0 + RET_DK] = ((x1 * sin + x2 * cos) * scale).astype(o_ref.dtype)
        o_ref[0, :, 2 * nk:] = u[:, 2 * nk:].astype(o_ref.dtype)
    elif mode == "att":
        c2 = rest[0][...]
        s2 = rest[1][...]
        qg = rest[2][...]
        kg = rest[3][...]
        nqk = ATT_HEADS + ATT_KV
        for hh in range(nqk):
            xh = u[:, hh * ATT_D:(hh + 1) * ATT_D]
            g = qg if hh < ATT_HEADS else kg
            r = lax.rsqrt(jnp.mean(xh * xh, axis=-1, keepdims=True) + NORM_EPS)
            xh = xh * r * g
            rot = xh * c2 + pltpu.roll(xh, ATT_D // 2, 1) * s2
            if hh < ATT_HEADS:
                rot = rot * (ATT_D ** -0.5 * math.log2(math.e))
            o_ref[0, :, hh * ATT_D:(hh + 1) * ATT_D] = rot.astype(o_ref.dtype)
        o_ref[0, :, nqk * ATT_D:] = u[:, nqk * ATT_D:].astype(o_ref.dtype)
    else:
        raise ValueError(mode)


def _norm_proj(x, modsel, norm_g, w_bf16, n_ctx_blk, mode="plain", extras=()):
    B, LT, D = x.shape
    F = w_bf16.shape[1]
    T = TOK_TILE
    tab_specs = []
    for e in extras:
        if e.shape[0] == 1:
            tab_specs.append(pl.BlockSpec(e.shape, lambda b, i: (0, 0)))
        else:
            tab_specs.append(pl.BlockSpec((T, e.shape[1]), lambda b, i: (i, 0)))
    return pl.pallas_call(
        functools.partial(_nmm_kernel, mode),
        grid=(B, LT // T),
        in_specs=[pl.BlockSpec((1, T, D), lambda b, i: (b, i, 0)),
                  pl.BlockSpec((1, 1, 3, D), lambda b, i: (jnp.where(i < n_ctx_blk, 0, 1), b, 0, 0)),
                  pl.BlockSpec((1, D), lambda b, i: (0, 0)),
                  pl.BlockSpec((D, F), lambda b, i: (0, 0), pipeline_mode=pl.Buffered(1)),
                  ] + tab_specs,
        out_specs=pl.BlockSpec((1, T, F), lambda b, i: (b, i, 0)),
        out_shape=jax.ShapeDtypeStruct((B, LT, F), BF16),
        compiler_params=_cparams(("arbitrary", "arbitrary")),
        name="norm_proj_" + mode,
    )(x, modsel, norm_g.reshape(1, D), w_bf16, *extras)


def _opr_kernel(z_ref, w_ref, x_ref, mod_ref, *rest):
    o_ref = rest[-1]
    y = _dot(z_ref[0], w_ref[...])
    x = x_ref[0] + mod_ref[0, 0][2:3] * y
    if len(rest) == 2:
        ms = jnp.mean(x * x, axis=-1, keepdims=True)
        x = x * lax.rsqrt(ms + NORM_EPS) * rest[0][...]
    o_ref[0] = x


def _out_proj_residual(z, w_bf16, x, modsel, n_ctx_blk, x_off_blk=0, final_g=None):
    B, Lz, Fz = z.shape
    D = x.shape[2]
    T = TOK_TILE
    in_specs = [pl.BlockSpec((1, T, Fz), lambda b, i: (b, i, 0)),
                pl.BlockSpec((Fz, D), lambda b, i: (0, 0), pipeline_mode=pl.Buffered(1)),
                pl.BlockSpec((1, T, D), lambda b, i: (b, i + x_off_blk, 0)),
                pl.BlockSpec((1, 1, 3, D),
                             lambda b, i: (jnp.where(i + x_off_blk < n_ctx_blk, 0, 1), b, 0, 0))]
    args = [z, w_bf16, x, modsel]
    if final_g is not None:
        in_specs.append(pl.BlockSpec((1, D), lambda b, i: (0, 0)))
        args.append(final_g.astype(F32).reshape(1, D))
    return pl.pallas_call(
        _opr_kernel,
        grid=(B, Lz // T),
        in_specs=in_specs,
        out_specs=pl.BlockSpec((1, T, D), lambda b, i: (b, i, 0)),
        out_shape=jax.ShapeDtypeStruct((B, Lz, D), F32),
        compiler_params=_cparams(("arbitrary", "arbitrary")),
        name="out_proj_residual",
    )(*args)


def _lru_kernel(fwd, T, n_ctx, n_tot, u_ref, up_ref, un_ref, cw_ref, cb_ref, wg_ref, bg_ref,
                lam_ref, *rest):
    if fwd:
        gate_ref, latb_ref, o_ref, h_sc, ext_sc, hs_sc = rest
    else:
        o_ref, h_sc, ext_sc = rest
    Wc = u_ref.shape[2]
    NB = SUBLANES
    HALO = up_ref.shape[1]
    s = pl.program_id(1)
    if fwd:
        c = s
    else:
        c = jnp.where(s < n_ctx, n_ctx - 1 - s, n_tot + n_ctx - 1 - s)

    @pl.when(s == 0)
    def _():
        h_sc[...] = jnp.zeros_like(h_sc)

    first = jnp.logical_or(c == 0, c == n_ctx)
    last = jnp.logical_or(c == n_ctx - 1, c == n_tot - 1)
    pm = jnp.where(first, 0.0, 1.0)
    nm = jnp.where(last, 0.0, 1.0)
    for b in range(NB):
        prev = up_ref[b, HALO - 2:HALO, :].astype(F32) * pm
        nxt = un_ref[b, 0:1, :].astype(F32) * nm
        for n in range(Wc // LRU_BLOCK):
            sl = slice(n * LRU_BLOCK, (n + 1) * LRU_BLOCK)
            ext_sc[n, pl.ds(b, 1), :] = prev[0:1, sl]
            ext_sc[n, pl.ds(NB + b, 1), :] = prev[1:2, sl]
            ext_sc[n, pl.ds(2 * NB + b, T, stride=NB), :] = u_ref[b, :, sl].astype(F32)
            ext_sc[n, pl.ds((T + 2) * NB + b, 1), :] = nxt[:, sl]
    lam = lam_ref[...]
    sp = jnp.maximum(-lam, 0.0) + jnp.log1p(jnp.exp(-jnp.abs(lam)))
    c2 = (-0.5 * LRU_C * math.log2(math.e)) * sp
    for n in range(Wc // LRU_BLOCK):
        sl = slice(n * LRU_BLOCK, (n + 1) * LRU_BLOCK)
        xn = cb_ref[:, sl] + cw_ref[0:1, sl] * ext_sc[n, 0:T * NB, :]
        for k in range(1, 4):
            xn = xn + cw_ref[k:k + 1, sl] * ext_sc[n, k * NB:(k + T) * NB, :]
        g = _dot(xn.astype(BF16), wg_ref[n])
        tr = jnp.tanh(g[:, :LRU_BLOCK] + bg_ref[0:1, sl])
        ti = jnp.tanh(g[:, LRU_BLOCK:] + bg_ref[1:2, sl])
        a = jnp.exp2(c2[:, sl] * tr + c2[:, sl])
        y = 1.0 - a * a
        xh = 0.5 * xn
        bb = jnp.where(y > 0.0, y * lax.rsqrt(y), 0.0) * (xh * ti + xh)
        a3 = a.reshape(T, SUBLANES, LRU_BLOCK)
        b3 = bb.reshape(T, SUBLANES, LRU_BLOCK)
        h = h_sc[:, sl]
        for t in (range(T) if fwd else range(T - 1, -1, -1)):
            h = a3[t] * h + b3[t]
            if fwd:
                hs_sc[n, t * NB:(t + 1) * NB, :] = h + latb_ref[t, :, sl]
            else:
                o_ref[t, :, sl] = h
        h_sc[:, sl] = h
        if fwd:
            for b in range(NB):
                hb = hs_sc[n, pl.ds(b, T, stride=NB), :]
                o_ref[b, :, sl] = (hb * _silu(gate_ref[b, :, sl].astype(F32))).astype(o_ref.dtype)


def _lru_pass(fwd, u, conv_w, conv_b, wg, bg, lam, n_ctx_rows, latb=None):
    B, LT, W2 = u.shape
    W = W2 // 2
    T, Wc = LRU_T, LRU_WC
    n_tot = LT // T
    n_ctx = n_ctx_rows // T
    nj = W // Wc
    halo = 2 * SUBLANES
    hb = T // halo

    def chunk(s):
        if fwd:
            return s
        return jnp.where(s < n_ctx, n_ctx - 1 - s, n_tot + n_ctx - 1 - s)

    in_specs = [
        pl.BlockSpec((B, T, Wc), lambda j, s: (0, chunk(s), j)),
        pl.BlockSpec((B, halo, Wc), lambda j, s: (0, jnp.maximum(chunk(s) * hb - 1, 0), j)),
        pl.BlockSpec((B, halo, Wc),
                     lambda j, s: (0, jnp.minimum((chunk(s) + 1) * hb, LT // halo - 1), j)),
        pl.BlockSpec((4, Wc), lambda j, s: (0, j)),
        pl.BlockSpec((1, Wc), lambda j, s: (0, j)),
        pl.BlockSpec((Wc // LRU_BLOCK, LRU_BLOCK, 2 * LRU_BLOCK), lambda j, s: (j, 0, 0)),
        pl.BlockSpec((2, Wc), lambda j, s: (0, j)),
        pl.BlockSpec((1, Wc), lambda j, s: (0, j)),
    ]
    args = [u, u, u, conv_w, conv_b.reshape(1, W), wg, bg, lam.reshape(1, W)]
    nb = Wc // LRU_BLOCK
    scratch = [pltpu.VMEM((B, Wc), F32), pltpu.VMEM((nb, (T + 3) * B, LRU_BLOCK), F32)]
    if fwd:
        in_specs += [pl.BlockSpec((B, T, Wc), lambda j, s: (0, s, nj + j)),
                     pl.BlockSpec((T, B, Wc), lambda j, s: (s, 0, j))]
        args += [u, latb]
        scratch.append(pltpu.VMEM((nb, T * B, LRU_BLOCK), F32))
        out_specs = pl.BlockSpec((B, T, Wc), lambda j, s: (0, s, j))
        out_shape = jax.ShapeDtypeStruct((B, LT, W), BF16)
    else:
        out_specs = pl.BlockSpec((T, B, Wc), lambda j, s: (chunk(s), 0, j))
        out_shape = jax.ShapeDtypeStruct((LT, B, W), F32)
    return pl.pallas_call(
        functools.partial(_lru_kernel, fwd, T, n_ctx, n_tot),
        grid=(nj, n_tot),
        in_specs=in_specs,
        out_specs=out_specs,
        out_shape=out_shape,
        scratch_shapes=scratch,
        compiler_params=_cparams(("arbitrary", "arbitrary")),
        name="lru_fwd" if fwd else "lru_bwd",
    )(*args)


def _ret_kernel(fwd, C, lg_ref, blk_ref, q_ref, k_ref, v_ref, *rest):
    if fwd:
        gate_ref, ob_ref, o_ref, st_sc, qdec_sc, kdec_sc, mask_sc = rest
    else:
        o_ref, st_sc, qdec_sc, kdec_sc = rest
    s = pl.program_id(1)
    d = 0 if fwd else 1

    @pl.when(s == 0)
    def _():
        st_sc[...] = jnp.zeros_like(st_sc)

    @pl.when(jnp.logical_and(pl.program_id(0) == 0, s == 0))
    def _():
        row_v = lax.broadcasted_iota(jnp.int32, (C, RET_DV), 0).astype(F32)
        row_k = lax.broadcasted_iota(jnp.int32, (C, RET_DK), 0).astype(F32)
        if fwd:
            ri = lax.broadcasted_iota(jnp.int32, (C, C), 0)
            ci = lax.broadcasted_iota(jnp.int32, (C, C), 1)
            diff = (ri - ci).astype(F32)
        for hh in range(RET_HEADS):
            lg = lg_ref[d, hh]
            if fwd:
                qdec_sc[hh] = jnp.exp((row_v + 1.0) * lg)
                kdec_sc[hh] = jnp.exp((C - 1.0 - row_k) * lg)
                mask_sc[hh] = jnp.exp(jnp.where(diff >= 0, diff * lg_ref[0, hh],
                                                -diff * lg_ref[1, hh]))
            else:
                qdec_sc[hh] = jnp.exp((C - row_v) * lg)
                kdec_sc[hh] = jnp.exp(row_k * lg)

    for hh in range(RET_HEADS):
        qh = q_ref[0, :, hh * RET_DK:(hh + 1) * RET_DK]
        kh = k_ref[0, :, hh * RET_DK:(hh + 1) * RET_DK]
        vh = v_ref[0, :, hh * RET_DV:(hh + 1) * RET_DV]
        st = st_sc[hh]
        o = _dot(qh, st.astype(BF16)) * qdec_sc[hh]
        kd = (kh.astype(F32) * kdec_sc[hh]).astype(BF16)
        st_sc[hh] = st * blk_ref[d, hh] + _dot_tn(kd, vh)
        if fwd:
            sc = _dot_nt(qh, kh) * mask_sc[hh]
            o = o + _dot(sc.astype(BF16), vh)
            o = o + ob_ref[0, :, hh * RET_DV:(hh + 1) * RET_DV].astype(F32)
            o = o * lax.rsqrt(jnp.mean(o * o, axis=-1, keepdims=True) + NORM_EPS)
            gate = gate_ref[0, :, hh * RET_DV:(hh + 1) * RET_DV].astype(F32)
            o_ref[0, :, hh * RET_DV:(hh + 1) * RET_DV] = (_silu(gate) * o).astype(o_ref.dtype)
        else:
            o_ref[0, :, hh * RET_DV:(hh + 1) * RET_DV] = o.astype(o_ref.dtype)


def _ret_pass(fwd, u, logg, blk, n_ctx_rows, ob=None):
    B, LT, _ = u.shape
    C = RET_CHUNK
    nk = RET_HEADS * RET_DK
    nv = RET_HEADS * RET_DV
    n_tot = LT // C
    n_ctx = n_ctx_rows // C

    def chunk(s):
        if fwd:
            return s
        return jnp.where(s < n_ctx, n_ctx - 1 - s, n_tot + n_ctx - 1 - s)

    smem = pl.BlockSpec(memory_space=pltpu.SMEM)
    in_specs = [smem, smem,
                pl.BlockSpec((1, C, nk), lambda b, s: (b, chunk(s), 0)),
                pl.BlockSpec((1, C, nk), lambda b, s: (b, chunk(s), 1)),
                pl.BlockSpec((1, C, nv), lambda b, s: (b, chunk(s), 1))]
    args = [logg, blk, u, u, u]
    if fwd:
        in_specs += [pl.BlockSpec((1, C, nv), lambda b, s: (b, s, 2)),
                     pl.BlockSpec((1, C, nv), lambda b, s: (b, s, 0))]
        args += [u, ob]
    return pl.pallas_call(
        functools.partial(_ret_kernel, fwd, C),
        grid=(B, n_tot),
        in_specs=in_specs,
        out_specs=pl.BlockSpec((1, C, nv), lambda b, s: (b, chunk(s), 0)),
        out_shape=jax.ShapeDtypeStruct((B, LT, nv), BF16),
        scratch_shapes=[pltpu.VMEM((RET_HEADS, RET_DK, RET_DV), F32),
                        pltpu.VMEM((RET_HEADS, C, RET_DV), F32),
                        pltpu.VMEM((RET_HEADS, C, RET_DK), F32)]
        + ([pltpu.VMEM((RET_HEADS, C, C), F32)] if fwd else []),
        compiler_params=_cparams(("arbitrary", "arbitrary")),
        name="ret_fwd" if fwd else "ret_bwd",
    )(*args)


def _hy_filter_kernel(z_ref, tt_ref, fw1_ref, fb1_ref, fw2_ref, fb2_ref, fq_ref, w3_ref,
                      delta_ref, o_ref):
    fq = fq_ref[...]
    h1 = jnp.sin(fq * (_dot(z_ref[...], fw1_ref[...], HIGHEST) + fb1_ref[...]))
    h2 = jnp.sin(fq * (_dot(h1, fw2_ref[...], HIGHEST) + fb2_ref[...]))
    kt = _dot_nt(w3_ref[0], h2, HIGHEST)
    t = tt_ref[0, 0:1, :]
    msk = tt_ref[0, 1:2, :]
    o_ref[0] = kt * jnp.exp(-delta_ref[...] * t) * msk


def _hy_circular(z, R):
    Lv = z.shape[0]
    N = R * LANES
    gap = N - 2 * Lv + 1
    zc = jnp.concatenate([z, jnp.zeros((gap, z.shape[1]), z.dtype), jnp.flip(z[1:], axis=0)], axis=0)
    valid = np.concatenate([np.ones(Lv), np.zeros(gap), np.ones(Lv - 1)]).astype(np.float32)
    return zc, valid


def _hy_filter_taps(Lv, R, fw1, fb1, fw2, fb2, fw3, freq):
    W = fw3.shape[1] // 4
    bands = (HY_EMB - 1) // 2
    t = jnp.linspace(0.0, 1.0, Lv, dtype=F32)[:, None]
    w = 2.0 * math.pi * jnp.arange(Lv, dtype=F32)[:, None] / Lv
    fr = jnp.linspace(1e-4, bands - 1, bands, dtype=F32)[None, :]
    z = jnp.concatenate([t, jnp.cos(fr * w), -jnp.sin(fr * w)], axis=-1)
    zc, valid = _hy_circular(z, R)
    zp = jnp.pad(zc, ((0, 0), (0, LANES - HY_EMB)))
    tt = jnp.stack([zc[:, 0].reshape(R, LANES), jnp.asarray(valid).reshape(R, LANES)], axis=1)
    fw1p = jnp.pad(fw1.astype(F32), ((0, LANES - HY_EMB), (0, 0)))
    w3 = fw3.astype(F32).reshape(HY_HIDDEN, 2, 2, W)
    w3d = jnp.stack([w3[:, :, 0, :].reshape(HY_HIDDEN, 2 * W).T,
                     w3[:, :, 1, :].reshape(HY_HIDDEN, 2 * W).T], axis=0)
    deltas = jnp.abs(jnp.linspace(math.log(HY_TARGET) / HY_SLOW_DECAY,
                                  math.log(HY_TARGET) / HY_FAST_DECAY, W, dtype=F32))
    delta2 = jnp.concatenate([deltas, deltas]).reshape(2 * W, 1)
    F2 = 2 * W
    half = R // 2
    kt = pl.pallas_call(
        _hy_filter_kernel,
        grid=(R,),
        in_specs=[pl.BlockSpec((LANES, LANES), lambda r: (r, 0)),
                  pl.BlockSpec((1, 2, LANES), lambda r: (r, 0, 0)),
                  pl.BlockSpec((LANES, HY_HIDDEN), lambda r: (0, 0)),
                  pl.BlockSpec((1, HY_HIDDEN), lambda r: (0, 0)),
                  pl.BlockSpec((HY_HIDDEN, HY_HIDDEN), lambda r: (0, 0)),
                  pl.BlockSpec((1, HY_HIDDEN), lambda r: (0, 0)),
                  pl.BlockSpec((1, HY_HIDDEN), lambda r: (0, 0)),
                  pl.BlockSpec((1, F2, HY_HIDDEN), lambda r: (jnp.where(r < half, 0, 1), 0, 0)),
                  pl.BlockSpec((F2, 1), lambda r: (0, 0))],
        out_specs=pl.BlockSpec((1, F2, LANES), lambda r: (r, 0, 0)),
        out_shape=jax.ShapeDtypeStruct((R, F2, LANES), F32),
        compiler_params=_cparams(("arbitrary",)),
        name="hyena_filter_taps",
    )(zp, tt, fw1p, fb1.astype(F32).reshape(1, -1), fw2.astype(F32), fb2.astype(F32).reshape(1, -1),
      freq.astype(F32).reshape(1, -1), w3d, delta2)
    return jnp.transpose(kt, (1, 0, 2))


def _dft_constants(R0, R):
    N = R * LANES
    k1 = np.arange(R)[:, None]
    n1 = np.arange(R0)[None, :]
    f1 = np.exp(-2j * np.pi * k1 * n1 / R)
    ma = np.block([[f1.real, -f1.imag], [f1.imag, f1.real]])
    g1 = np.exp(2j * np.pi * n1.T * k1.T / R) / N
    mai = np.block([[g1.real, -g1.imag], [g1.imag, g1.real]])
    n2 = np.arange(LANES)
    tw = np.exp(-2j * np.pi * np.arange(R)[:, None] * n2[None, :] / N)
    f2 = np.exp(-2j * np.pi * n2[:, None] * n2[None, :] / LANES)
    mc = np.block([[f2.real, f2.imag], [-f2.imag, f2.real]])
    mci = np.block([[f2.real, -f2.imag], [f2.imag, f2.real]])
    f1_full = np.exp(-2j * np.pi * k1 * np.arange(R)[None, :] / R)
    as_bf = lambda m: jnp.asarray(m, F32).astype(BF16)
    return dict(ma=as_bf(ma), mai=as_bf(mai), mc=as_bf(mc), mci=as_bf(mci),
                tr=jnp.asarray(tw.real, F32), ti=jnp.asarray(tw.imag, F32),
                ma_real=as_bf(np.concatenate([f1_full.real, f1_full.imag], axis=0)))


def _hy_fft_kernel(R, cb, k_ref, ma_ref, tr_ref, ti_ref, mc_ref, or_ref, oi_ref):
    cols = [k_ref[c] for c in range(cb)]
    inv = []
    for col in cols:
        sabs = jnp.sum(jnp.sum(jnp.abs(col), axis=1, keepdims=True), axis=0, keepdims=True)
        inv.append(1.0 / sabs)
    rhs = jnp.concatenate(cols, axis=1)
    hi = rhs.astype(BF16)
    lo = (rhs - hi.astype(F32)).astype(BF16)
    a = _dot(ma_ref[...], hi) + _dot(ma_ref[...], lo)
    tr = tr_ref[...]
    ti = ti_ref[...]
    rows = []
    for c in range(cb):
        ar = a[:R, c * LANES:(c + 1) * LANES]
        ai = a[R:, c * LANES:(c + 1) * LANES]
        rows.append(jnp.concatenate([ar * tr - ai * ti, ar * ti + ai * tr], axis=1))
    lhs = jnp.concatenate(rows, axis=0)
    hi = lhs.astype(BF16)
    lo = (lhs - hi.astype(F32)).astype(BF16)
    x = _dot(hi, mc_ref[...]) + _dot(lo, mc_ref[...])
    for c in range(cb):
        or_ref[c] = x[c * R:(c + 1) * R, :LANES] * inv[c]
        oi_ref[c] = x[c * R:(c + 1) * R, LANES:] * inv[c]


def _hy_filter_spectrum(kt, R, consts):
    F2 = kt.shape[0]
    cb = HY_CB_SHORT
    spec = pl.BlockSpec((cb, R, LANES), lambda j: (j, 0, 0))
    full = lambda a: pl.BlockSpec(a.shape, lambda j: (0,) * a.ndim)
    return pl.pallas_call(
        functools.partial(_hy_fft_kernel, R, cb),
        grid=(F2 // cb,),
        in_specs=[spec, full(consts["ma_real"]), full(consts["tr"]), full(consts["ti"]),
                  full(consts["mc"])],
        out_specs=[spec, spec],
        out_shape=[jax.ShapeDtypeStruct((F2, R, LANES), F32)] * 2,
        compiler_params=_cparams(("arbitrary",)),
        name="hyena_filter_fft",
    )(kt, consts["ma_real"], consts["tr"], consts["ti"], consts["mc"])


def _hy_conv_kernel(R0, R, Lv, cb, W, cw_ref, cbias_ref, skip_ref, v_ref, x1_ref, x2_ref, g_ref,
                    k0r_ref, k0i_ref, k1r_ref, k1i_ref, ma_ref, mai_ref, mc_ref, mci_ref,
                    tr_ref, ti_ref, o_ref):
    j = pl.program_id(0)
    row = lax.broadcasted_iota(jnp.int32, (R0, LANES), 0)
    lane = lax.broadcasted_iota(jnp.int32, (R0, LANES), 1)
    pos = row * LANES + lane
    tr = tr_ref[...]
    ti = ti_ref[...]

    def short_conv(ref, part, c, grp):
        x = ref[0, part, c].astype(F32)
        ch = grp * W + j * cb + c
        r = pltpu.roll(x, 1, 1)
        prev = jnp.where(lane == 0, pltpu.roll(r, 1, 0), r)
        prev = jnp.where(pos == 0, 0.0, prev)
        r = pltpu.roll(x, LANES - 1, 1)
        nxt = jnp.where(lane == LANES - 1, pltpu.roll(r, R0 - 1, 0), r)
        nxt = jnp.where(pos == R0 * LANES - 1, 0.0, nxt)
        y = cw_ref[0, ch] * prev + cw_ref[1, ch] * x + cw_ref[2, ch] * nxt + cbias_ref[ch]
        if Lv < R0 * LANES:
            y = jnp.where(pos < Lv, y, 0.0)
        return y

    def long_conv(re, im, kr_ref, ki_ref):
        rhs = jnp.concatenate([jnp.concatenate([re[c], im[c]], axis=0) for c in range(cb)], axis=1)
        a = _dot(ma_ref[...], rhs.astype(BF16))
        rows = []
        for c in range(cb):
            ar = a[:R, c * LANES:(c + 1) * LANES]
            ai = a[R:, c * LANES:(c + 1) * LANES]
            rows.append(jnp.concatenate([ar * tr - ai * ti, ar * ti + ai * tr], axis=1))
        x = _dot(jnp.concatenate(rows, axis=0).astype(BF16), mc_ref[...])
        rows = []
        for c in range(cb):
            xr = x[c * R:(c + 1) * R, :LANES]
            xi = x[c * R:(c + 1) * R, LANES:]
            kr = kr_ref[c]
            ki = ki_ref[c]
            rows.append(jnp.concatenate([xr * kr - xi * ki, xr * ki + xi * kr], axis=1))
        b = _dot(jnp.concatenate(rows, axis=0).astype(BF16), mci_ref[...])
        cols = []
        for c in range(cb):
            br = b[c * R:(c + 1) * R, :LANES]
            bi = b[c * R:(c + 1) * R, LANES:]
            cols.append(jnp.concatenate([br * tr + bi * ti, bi * tr - br * ti], axis=0))
        y = _dot(mai_ref[...], jnp.concatenate(cols, axis=1).astype(BF16))
        out_re = [y[:R0, c * LANES:(c + 1) * LANES] for c in range(cb)]
        out_im = [y[R0:, c * LANES:(c + 1) * LANES] for c in range(cb)]
        return out_re, out_im

    v = [[short_conv(v_ref, p, c, 0) for c in range(cb)] for p in range(2)]
    x1 = [[short_conv(x1_ref, p, c, 1) for c in range(cb)] for p in range(2)]
    cr, ci = long_conv(v[0], v[1], k0r_ref, k0i_ref)
    conv = (cr, ci)
    y = [[x1[p][c] * (conv[p][c] + v[p][c] * skip_ref[0, j * cb + c]) for c in range(cb)]
         for p in range(2)]
    x2 = [[short_conv(x2_ref, p, c, 2) for c in range(cb)] for p in range(2)]
    cr, ci = long_conv(y[0], y[1], k1r_ref, k1i_ref)
    conv = (cr, ci)
    for p in range(2):
        for c in range(cb):
            y2 = x2[p][c] * (conv[p][c] + y[p][c] * skip_ref[1, j * cb + c])
            gate = g_ref[0, p, c].astype(F32)
            o_ref[0, p, c] = (y2 * _silu(gate)).astype(o_ref.dtype)


def _hy_conv(ut, kfr, kfi, conv_w, conv_b, skip, Lv, R0, R, consts):
    G, _, W4, _, _ = ut.shape
    W = W4 // 4
    cb = HY_CB if R0 * LANES >= HY_SHORT_SEQ else HY_CB_SHORT
    nj = W // cb
    smem = pl.BlockSpec(memory_space=pltpu.SMEM)
    full = lambda a: pl.BlockSpec(a.shape, lambda j, g: (0,) * a.ndim)

    def data(grp):
        return pl.BlockSpec((1, 2, cb, R0, LANES), lambda j, g: (g, 0, grp * nj + j, 0, 0))

    def filt(order):
        return pl.BlockSpec((cb, R, LANES), lambda j, g: (order * nj + j, 0, 0))

    return pl.pallas_call(
        functools.partial(_hy_conv_kernel, R0, R, Lv, cb, W),
        grid=(nj, G),
        in_specs=[smem, smem, smem, data(0), data(1), data(2), data(3),
                  filt(0), filt(0), filt(1), filt(1),
                  full(consts["ma"]), full(consts["mai"]), full(consts["mc"]), full(consts["mci"]),
                  full(consts["tr"]), full(consts["ti"])],
        out_specs=pl.BlockSpec((1, 2, cb, R0, LANES), lambda j, g: (g, 0, j, 0, 0)),
        out_shape=jax.ShapeDtypeStruct((G, 2, W, R0, LANES), BF16),
        compiler_params=_cparams(("arbitrary", "arbitrary")),
        name="hyena_conv",
    )(conv_w, conv_b, skip, ut, ut, ut, ut, kfr, kfi, kfr, kfi,
      consts["ma"], consts["mai"], consts["mc"], consts["mci"], consts["tr"], consts["ti"])


def _hy_segment(u_seg, conv_w, conv_b, fw1, fb1, fw2, fb2, fw3, freq, skip):
    B, rows, _, W4 = u_seg.shape
    W = W4 // 4
    Lv = rows * LANES
    R0 = -(-rows // SUBLANES) * SUBLANES
    R = max(R0, -(-(2 * rows) // SUBLANES) * SUBLANES)
    consts = _dft_constants(R0, R)
    kt = _hy_filter_taps(Lv, R, fw1, fb1, fw2, fb2, fw3, freq)
    kfr, kfi = _hy_filter_spectrum(kt, R, consts)
    ut = jnp.transpose(u_seg, (0, 3, 1, 2))
    if R0 > rows:
        ut = jnp.pad(ut, ((0, 0), (0, 0), (0, R0 - rows), (0, 0)))
    ut = ut.reshape(B // 2, 2, W4, R0, LANES)
    z = _hy_conv(ut, kfr, kfi, conv_w.astype(F32), conv_b.astype(F32), skip.astype(F32),
                 Lv, R0, R, consts)
    z = z.reshape(B, W, R0, LANES)[:, :, :rows]
    return jnp.transpose(z, (0, 2, 3, 1)).reshape(B, Lv, W)


def _attn_kernel(n_chunks, ck, q0_ref, q1_ref, q2_ref, k_ref, v_ref, g_ref, o_ref,
                 sa_sc, sb_sc, ma_sc):
    R = ATT_HEADS // ATT_KV
    tq = q1_ref.shape[1]

    def stack_heads(q_ref):
        return jnp.concatenate([q_ref[0, :, r * ATT_D:(r + 1) * ATT_D] for r in range(R)], axis=0)

    ncb = ck // LANES

    def scores(q, s_sc, c):
        s = _dot_nt(q, k_ref[0, c * ck:(c + 1) * ck, :])
        s_sc[:, c * ck:(c + 1) * ck] = s
        mp = s[:, :LANES]
        for j in range(1, ncb):
            mp = jnp.maximum(mp, s[:, j * LANES:(j + 1) * LANES])
        return mp

    def weighted(s_sc, m_rep, c):
        ps = [jnp.exp2(s_sc[:, c * ck + j * LANES:c * ck + (j + 1) * LANES] - m_rep)
              for j in range(ncb)]
        lp = ps[0]
        for j in range(1, ncb):
            lp = lp + ps[j]
        p = jnp.concatenate([pj.astype(BF16) for pj in ps], axis=1)
        return lp, _dot(p, v_ref[0, c * ck:(c + 1) * ck, :])

    def lane_max_rep(mp):
        return jnp.broadcast_to(jnp.max(mp, axis=-1, keepdims=True), mp.shape)

    def finish(acc, lp, half):
        o = acc / jnp.sum(lp, axis=-1, keepdims=True)
        for r in range(R):
            rows = slice(half * tq, (half + 1) * tq)
            gate = g_ref[0, rows, r * ATT_D:(r + 1) * ATT_D].astype(F32)
            o_ref[0, rows, r * ATT_D:(r + 1) * ATT_D] = (
                _silu(gate) * o[r * tq:(r + 1) * tq]).astype(o_ref.dtype)

    def phase(s_cur, m_cur, q_next, s_next):
        m_next = l = acc = None
        for c in range(n_chunks):
            mc = scores(q_next, s_next, c)
            lc, pv = weighted(s_cur, m_cur, c)
            m_next = mc if m_next is None else jnp.maximum(m_next, mc)
            l = lc if l is None else l + lc
            acc = pv if acc is None else acc + pv
        return acc, l, lane_max_rep(m_next)

    @pl.when(pl.program_id(2) == 0)
    def _():
        q0 = stack_heads(q0_ref)
        m = None
        for c in range(n_chunks):
            mc = scores(q0, sa_sc, c)
            m = mc if m is None else jnp.maximum(m, mc)
        ma_sc[...] = lane_max_rep(m)

    acc, l, mb = phase(sa_sc, ma_sc[...], stack_heads(q1_ref), sb_sc)
    finish(acc, l, 0)
    acc, l, ma = phase(sb_sc, mb, stack_heads(q2_ref), sa_sc)
    finish(acc, l, 1)
    ma_sc[...] = ma


def _attention(u, n_ctx_rows):
    B, LT, _ = u.shape
    L = LT - n_ctx_rows
    R = ATT_HEADS // ATT_KV
    tq = ATT_TQ
    ck = max(t for t in range(LANES, ATT_CHUNK_MAX + 1, LANES) if LT % t == 0)
    qw = R * ATT_D
    n_tiles = L // tq
    assert n_tiles % 2 == 0 and n_ctx_rows % (2 * tq) == 0
    q_off = n_ctx_rows // tq
    k_col = ATT_HEADS
    v_col = ATT_HEADS + ATT_KV
    g_col = (ATT_HEADS + 2 * ATT_KV) * ATT_D // qw

    def q_spec(tile):
        return pl.BlockSpec((1, tq, qw), lambda b, g, i: (b, q_off + tile(i), g))

    return pl.pallas_call(
        functools.partial(_attn_kernel, LT // ck, ck),
        grid=(B, ATT_KV, n_tiles // 2),
        in_specs=[q_spec(lambda i: 0),
                  q_spec(lambda i: 2 * i + 1),
                  q_spec(lambda i: jnp.minimum(2 * i + 2, n_tiles - 1)),
                  pl.BlockSpec((1, LT, ATT_D), lambda b, g, i: (b, 0, k_col + g),
                               pipeline_mode=pl.Buffered(1)),
                  pl.BlockSpec((1, LT, ATT_D), lambda b, g, i: (b, 0, v_col + g),
                               pipeline_mode=pl.Buffered(1)),
                  pl.BlockSpec((1, 2 * tq, qw), lambda b, g, i: (b, i + q_off // 2, g_col + g))],
        out_specs=pl.BlockSpec((1, 2 * tq, qw), lambda b, g, i: (b, i, g)),
        out_shape=jax.ShapeDtypeStruct((B, L, ATT_HEADS * ATT_D), BF16),
        scratch_shapes=[pltpu.VMEM((R * tq, LT), F32), pltpu.VMEM((R * tq, LT), F32),
                        pltpu.VMEM((R * tq, LANES), F32)],
        compiler_params=_cparams(("arbitrary", "arbitrary", "arbitrary")),
        name="attention",
    )(u, u, u, u, u, u)


def _rope_angles(pos, dim):
    half = dim // 2
    inv = ROPE_THETA ** (-jnp.arange(half, dtype=F32) / half)
    return pos[:, None] * inv[None, :]


def _with_ctx_identity(cos_like, sin_like, n_ctx_rows):
    ones = jnp.ones((n_ctx_rows, cos_like.shape[1]), F32)
    return (jnp.concatenate([ones, cos_like], axis=0),
            jnp.concatenate([jnp.zeros_like(ones), sin_like], axis=0))


def _layer_lru(xs, Lc, ms, norm_g, w_in, conv_w, conv_b, w_r, b_r, w_i, b_i, lam, w_out):
    n_ctx_blk = Lc // TOK_TILE
    u = _norm_proj(xs, ms, norm_g, w_in.astype(BF16), n_ctx_blk)
    lat = None
    for d in (1, 0):
        wg = (0.5 * jnp.concatenate([w_r[d], w_i[d]], axis=-1)).astype(BF16)
        bg = 0.5 * jnp.stack([b_r[d], b_i[d]], axis=0).astype(F32)
        lat = _lru_pass(d == 0, u, conv_w.astype(F32), conv_b.astype(F32), wg, bg,
                        lam[d].astype(F32), Lc, latb=lat)
    return _out_proj_residual(lat, w_out.astype(BF16), xs, ms, n_ctx_blk)


def _layer_ret(xs, Lc, ms, norm_g, w_in, decay_logit, w_out):
    n_ctx_blk = Lc // TOK_TILE
    L = xs.shape[1] - Lc
    ang = _rope_angles(jnp.arange(L, dtype=F32), RET_DK)
    cos, sin = _with_ctx_identity(jnp.cos(ang), jnp.sin(ang), Lc)
    u = _norm_proj(xs, ms, norm_g, w_in.astype(BF16), n_ctx_blk, "ret", (cos, sin))
    dl = decay_logit.astype(F32)
    logg = -(jnp.maximum(-dl, 0.0) + jnp.log1p(jnp.exp(-jnp.abs(dl))))
    blk = jnp.exp(RET_CHUNK * logg)
    ob = _ret_pass(False, u, logg, blk, Lc)
    z = _ret_pass(True, u, logg, blk, Lc, ob=ob)
    return _out_proj_residual(z, w_out.astype(BF16), xs, ms, n_ctx_blk)


def _layer_hyena(xs, Lc, ms, norm_g, w_in, conv_w, conv_b, fw1, fb1, fw2, fb2, fw3, freq, skip,
                 w_out):
    n_ctx_blk = Lc // TOK_TILE
    L = xs.shape[1] - Lc
    u = _norm_proj(xs, ms, norm_g, w_in.astype(BF16), n_ctx_blk)
    hy_args = (conv_w, conv_b, fw1, fb1, fw2, fb2, fw3, freq, skip)
    assert Lc % LANES == 0 and L % LANES == 0
    u4 = u.reshape(u.shape[0], (Lc + L) // LANES, LANES, u.shape[2])
    z = jnp.concatenate([_hy_segment(u4[:, :Lc // LANES], *hy_args),
                         _hy_segment(u4[:, Lc // LANES:], *hy_args)], axis=1)
    return _out_proj_residual(z, w_out.astype(BF16), xs, ms, n_ctx_blk)


def _layer_attn(xs, Lc, ms, norm_g, w_in, q_norm_g, k_norm_g, w_out, final_g=None):
    n_ctx_blk = Lc // TOK_TILE
    L = xs.shape[1] - Lc
    rows = L // GRID_W
    row = jnp.repeat(jnp.arange(rows, dtype=F32), GRID_W)
    col = jnp.tile(jnp.arange(GRID_W, dtype=F32), rows)
    ang = jnp.concatenate([_rope_angles(row, ATT_D // 2), _rope_angles(col, ATT_D // 2)], axis=-1)
    cos, sin = _with_ctx_identity(jnp.cos(ang), jnp.sin(ang), Lc)
    c2 = jnp.concatenate([cos, cos], axis=-1)
    s2 = jnp.concatenate([-sin, sin], axis=-1)
    u = _norm_proj(xs, ms, norm_g, w_in.astype(BF16), n_ctx_blk, "att",
                   (c2, s2, q_norm_g.astype(F32).reshape(1, ATT_D),
                    k_norm_g.astype(F32).reshape(1, ATT_D)))
    z = _attention(u, Lc)
    return _out_proj_residual(z, w_out.astype(BF16), xs, ms, n_ctx_blk, x_off_blk=n_ctx_blk,
                              final_g=final_g)


def kernel(x, c, ctx, c_ctx, lru_mod_w, lru_mod_b, lru_norm_g, lru_w_in, lru_conv_w, lru_conv_b, lru_w_r, lru_b_r, lru_w_i, lru_b_i, lru_lambda, lru_w_out, ret_mod_w, ret_mod_b, ret_norm_g, ret_w_in, ret_decay_logit, ret_w_out, hy_mod_w, hy_mod_b, hy_norm_g, hy_w_in, hy_conv_w, hy_conv_b, hy_fw1, hy_fb1, hy_fw2, hy_fb2, hy_fw3, hy_freq, hy_skip, hy_w_out, att_mod_w, att_mod_b, att_norm_g, att_w_in, att_q_norm_g, att_k_norm_g, att_w_out, final_norm_g):
    B, L, D = x.shape
    Lc = ctx.shape[1]
    LT = Lc + L
    assert B == SUBLANES and B % 2 == 0
    assert Lc % TOK_TILE == 0 and L % TOK_TILE == 0 and Lc % RET_CHUNK == 0 and Lc % LRU_T == 0
    n_ctx_blk = Lc // TOK_TILE

    cvec = jnp.zeros((16, D), F32).at[:B].set(c).at[B].set(c_ctx)

    def modsel(mod_w, mod_b):
        m = _modulation(cvec, mod_w, mod_b).reshape(16, 3, D)
        return jnp.stack([jnp.broadcast_to(m[B][None], (B, 3, D)), m[:B]], axis=0)

    xs = jnp.concatenate([ctx, x], axis=1)
    xs = _layer_lru(xs, Lc, modsel(lru_mod_w, lru_mod_b), lru_norm_g, lru_w_in, lru_conv_w,
                    lru_conv_b, lru_w_r, lru_b_r, lru_w_i, lru_b_i, lru_lambda, lru_w_out)
    xs = _layer_ret(xs, Lc, modsel(ret_mod_w, ret_mod_b), ret_norm_g, ret_w_in, ret_decay_logit,
                    ret_w_out)
    xs = _layer_hyena(xs, Lc, modsel(hy_mod_w, hy_mod_b), hy_norm_g, hy_w_in, hy_conv_w, hy_conv_b,
                      hy_fw1, hy_fb1, hy_fw2, hy_fb2, hy_fw3, hy_freq, hy_skip, hy_w_out)
    return _layer_attn(xs, Lc, modsel(att_mod_w, att_mod_b), att_norm_g, att_w_in, att_q_norm_g,
                       att_k_norm_g, att_w_out, final_g=final_norm_g)
```

```python
import functools
import math

import numpy as np
import jax
import jax.numpy as jnp
from jax import lax
from jax.experimental import pallas as pl
from jax.experimental.pallas import tpu as pltpu

F32 = jnp.float32
BF16 = jnp.bfloat16
HIGHEST = lax.Precision.HIGHEST

NORM_EPS = 1e-6
ROPE_THETA = 10000.0
GRID_W = 64
LANES = 128
SUBLANES = 8
VMEM_LIMIT = 59 * 1024 * 1024

TOK_TILE = 256

LRU_C = 8.0
LRU_BLOCK = 128
LRU_T = 128
LRU_WC = 512

RET_HEADS = 4
RET_DK = 256
RET_DV = 512
RET_CHUNK = 256

HY_EMB = 33
HY_HIDDEN = 64
HY_FAST_DECAY = 0.3
HY_SLOW_DECAY = 1.5
HY_TARGET = 1e-2
HY_CB = 8
HY_CB_SHORT = 32
HY_SHORT_SEQ = 4096
HY_TAP_ROWS = 4

ATT_HEADS = 8
ATT_KV = 2
ATT_D = 128
ATT_TQ = 128
ATT_CHUNK_MAX = 256


def _cparams(sem):
    return pltpu.CompilerParams(dimension_semantics=sem, vmem_limit_bytes=VMEM_LIMIT)


def _dot(a, b, precision=None):
    return lax.dot_general(a, b, (((1,), (0,)), ((), ())), precision=precision,
                           preferred_element_type=F32)


def _dot_nt(a, b, precision=None):
    return lax.dot_general(a, b, (((1,), (1,)), ((), ())), precision=precision,
                           preferred_element_type=F32)


def _dot_tn(a, b):
    return lax.dot_general(a, b, (((0,), (0,)), ((), ())), preferred_element_type=F32)


def _silu(x):
    xh = 0.5 * x
    return xh * jnp.tanh(xh) + xh


def _mod_kernel(c_ref, w_ref, b_ref, o_ref):
    o_ref[...] = _dot(_silu(c_ref[...]), w_ref[...], HIGHEST) + b_ref[...]


def _modulation(cvec, mod_w, mod_b):
    D = cvec.shape[1]
    return pl.pallas_call(
        _mod_kernel,
        grid=(3,),
        in_specs=[pl.BlockSpec((16, D), lambda n: (0, 0)),
                  pl.BlockSpec((D, D), lambda n: (0, n)),
                  pl.BlockSpec((1, D), lambda n: (0, n))],
        out_specs=pl.BlockSpec((16, D), lambda n: (0, n)),
        out_shape=jax.ShapeDtypeStruct((16, 3 * D), F32),
        compiler_params=_cparams(("arbitrary",)),
        name="modulation",
    )(cvec, mod_w, mod_b.reshape(1, 3 * D))


def _nmm_kernel(mode, x_ref, mod_ref, g_ref, w_ref, *rest):
    o_ref = rest[-1]
    x = x_ref[0]
    ms = jnp.mean(x * x, axis=-1, keepdims=True)
    xn = x * lax.rsqrt(ms + NORM_EPS) * g_ref[...]
    mod = mod_ref[0, 0]
    h = xn * (1.0 + mod[1:2]) + mod[0:1]
    u = _dot(h.astype(BF16), w_ref[...])
    if mode == "plain":
        o_ref[0] = u.astype(o_ref.dtype)
    elif mode == "ret":
        cos = rest[0][...]
        sin = rest[1][...]
        nk = RET_HEADS * RET_DK
        half = RET_DK // 2
        for base, scale in ((0, 1.0), (nk, RET_DK ** -0.5)):
            for hh in range(RET_HEADS):
                c0 = base + hh * RET_DK
                x1 = u[:, c0:c0 + half]
                x2 = u[:, c0 + half:c0 + RET_DK]
                o_ref[0, :, c0:c0 + half] = ((x1 * cos - x2 * sin) * scale).astype(o_ref.dtype)
                o_ref[0, :, c0 + half:c0 + RET_DK] = ((x1 * sin + x2 * cos) * scale).astype(o_ref.dtype)
        o_ref[0, :, 2 * nk:] = u[:, 2 * nk:].astype(o_ref.dtype)
    elif mode == "att":
        c2 = rest[0][...]
        s2 = rest[1][...]
        qg = rest[2][...]
        kg = rest[3][...]
        nqk = ATT_HEADS + ATT_KV
        for hh in range(nqk):
            xh = u[:, hh * ATT_D:(hh + 1) * ATT_D]
            g = qg if hh < ATT_HEADS else kg
            r = lax.rsqrt(jnp.mean(xh * xh, axis=-1, keepdims=True) + NORM_EPS)
            xh = xh * r * g
            rot = xh * c2 + pltpu.roll(xh, ATT_D // 2, 1) * s2
            if hh < ATT_HEADS:
                rot = rot * (ATT_D ** -0.5 * math.log2(math.e))
            o_ref[0, :, hh * ATT_D:(hh + 1) * ATT_D] = rot.astype(o_ref.dtype)
        o_ref[0, :, nqk * ATT_D:] = u[:, nqk * ATT_D:].astype(o_ref.dtype)
    else:
        raise ValueError(mode)


def _norm_proj(x, modsel, norm_g, w_bf16, n_ctx_blk, mode="plain", extras=(), ctx_last=False):
    B, LT, D = x.shape
    F = w_bf16.shape[1]
    T = TOK_TILE
    n_blk = LT // T

    def out_blk(i):
        if not ctx_last:
            return i
        return jnp.where(i < n_ctx_blk, i + n_blk - n_ctx_blk, i - n_ctx_blk)

    tab_specs = []
    for e in extras:
        if e.shape[0] == 1:
            tab_specs.append(pl.BlockSpec(e.shape, lambda b, i: (0, 0)))
        else:
            tab_specs.append(pl.BlockSpec((T, e.shape[1]), lambda b, i: (i, 0)))
    return pl.pallas_call(
        functools.partial(_nmm_kernel, mode),
        grid=(B, LT // T),
        in_specs=[pl.BlockSpec((1, T, D), lambda b, i: (b, i, 0)),
                  pl.BlockSpec((1, 1, 3, D), lambda b, i: (jnp.where(i < n_ctx_blk, 0, 1), b, 0, 0)),
                  pl.BlockSpec((1, D), lambda b, i: (0, 0)),
                  pl.BlockSpec((D, F), lambda b, i: (0, 0), pipeline_mode=pl.Buffered(1)),
                  ] + tab_specs,
        out_specs=pl.BlockSpec((1, T, F), lambda b, i: (b, out_blk(i), 0)),
        out_shape=jax.ShapeDtypeStruct((B, LT, F), BF16),
        compiler_params=_cparams(("arbitrary", "arbitrary")),
        name="norm_proj_" + mode,
    )(x, modsel, norm_g.reshape(1, D), w_bf16, *extras)


def _opr_kernel(z_ref, w_ref, x_ref, mod_ref, *rest):
    o_ref = rest[-1]
    y = _dot(z_ref[0], w_ref[...])
    x = x_ref[0] + mod_ref[0, 0][2:3] * y
    if len(rest) == 2:
        ms = jnp.mean(x * x, axis=-1, keepdims=True)
        x = x * lax.rsqrt(ms + NORM_EPS) * rest[0][...]
    o_ref[0] = x


def _out_proj_residual(z, w_bf16, x, modsel, n_ctx_blk, x_off_blk=0, final_g=None):
    B, Lz, Fz = z.shape
    D = x.shape[2]
    T = TOK_TILE
    in_specs = [pl.BlockSpec((1, T, Fz), lambda b, i: (b, i, 0)),
                pl.BlockSpec((Fz, D), lambda b, i: (0, 0), pipeline_mode=pl.Buffered(1)),
                pl.BlockSpec((1, T, D), lambda b, i: (b, i + x_off_blk, 0)),
                pl.BlockSpec((1, 1, 3, D),
                             lambda b, i: (jnp.where(i + x_off_blk < n_ctx_blk, 0, 1), b, 0, 0))]
    args = [z, w_bf16, x, modsel]
    if final_g is not None:
        in_specs.append(pl.BlockSpec((1, D), lambda b, i: (0, 0)))
        args.append(final_g.astype(F32).reshape(1, D))
    return pl.pallas_call(
        _opr_kernel,
        grid=(B, Lz // T),
        in_specs=in_specs,
        out_specs=pl.BlockSpec((1, T, D), lambda b, i: (b, i, 0)),
        out_shape=jax.ShapeDtypeStruct((B, Lz, D), F32),
        compiler_params=_cparams(("arbitrary", "arbitrary")),
        name="out_proj_residual",
    )(*args)


def _lru_kernel(fwd, T, n_ctx, n_tot, u_ref, up_ref, un_ref, cw_ref, cb_ref, wg_ref, bg_ref,
                lam_ref, *rest):
    if fwd:
        gate_ref, latb_ref, o_ref, h_sc, ext_sc, hs_sc = rest
    else:
        o_ref, h_sc, ext_sc = rest
    Wc = u_ref.shape[2]
    NB = SUBLANES
    HALO = up_ref.shape[1]
    s = pl.program_id(1)
    if fwd:
        c = s
    else:
        c = jnp.where(s < n_ctx, n_ctx - 1 - s, n_tot + n_ctx - 1 - s)

    @pl.when(s == 0)
    def _():
        h_sc[...] = jnp.zeros_like(h_sc)

    first = jnp.logical_or(c == 0, c == n_ctx)
    last = jnp.logical_or(c == n_ctx - 1, c == n_tot - 1)
    pm = jnp.where(first, 0.0, 1.0)
    nm = jnp.where(last, 0.0, 1.0)
    for b in range(NB):
        prev = up_ref[b, HALO - 2:HALO, :].astype(F32) * pm
        nxt = un_ref[b, 0:1, :].astype(F32) * nm
        for n in range(Wc // LRU_BLOCK):
            sl = slice(n * LRU_BLOCK, (n + 1) * LRU_BLOCK)
            ext_sc[n, pl.ds(b, 1), :] = prev[0:1, sl]
            ext_sc[n, pl.ds(NB + b, 1), :] = prev[1:2, sl]
            ext_sc[n, pl.ds(2 * NB + b, T, stride=NB), :] = u_ref[b, :, sl].astype(F32)
            ext_sc[n, pl.ds((T + 2) * NB + b, 1), :] = nxt[:, sl]
    lam = lam_ref[...]
    sp = jnp.maximum(-lam, 0.0) + jnp.log1p(jnp.exp(-jnp.abs(lam)))
    c2 = (-0.5 * LRU_C * math.log2(math.e)) * sp
    for n in range(Wc // LRU_BLOCK):
        sl = slice(n * LRU_BLOCK, (n + 1) * LRU_BLOCK)
        xn = cb_ref[:, sl] + cw_ref[0:1, sl] * ext_sc[n, 0:T * NB, :]
        for k in range(1, 4):
            xn = xn + cw_ref[k:k + 1, sl] * ext_sc[n, k * NB:(k + T) * NB, :]
        g = _dot(xn.astype(BF16), wg_ref[n])
        tr = jnp.tanh(g[:, :LRU_BLOCK] + bg_ref[0:1, sl])
        ti = jnp.tanh(g[:, LRU_BLOCK:] + bg_ref[1:2, sl])
        a = jnp.exp2(c2[:, sl] * tr + c2[:, sl])
        y = 1.0 - a * a
        xh = 0.5 * xn
        bb = jnp.where(y > 0.0, y * lax.rsqrt(y), 0.0) * (xh * ti + xh)
        a3 = a.reshape(T, SUBLANES, LRU_BLOCK)
        b3 = bb.reshape(T, SUBLANES, LRU_BLOCK)
        h = h_sc[:, sl]
        for t in (range(T) if fwd else range(T - 1, -1, -1)):
            h = a3[t] * h + b3[t]
            if fwd:
                hs_sc[n, t * NB:(t + 1) * NB, :] = h + latb_ref[t, :, sl]
            else:
                o_ref[t, :, sl] = h
        h_sc[:, sl] = h
        if fwd:
            for b in range(NB):
                hb = hs_sc[n, pl.ds(b, T, stride=NB), :]
                o_ref[b, :, sl] = (hb * _silu(gate_ref[b, :, sl].astype(F32))).astype(o_ref.dtype)


def _lru_pass(fwd, u, conv_w, conv_b, wg, bg, lam, n_ctx_rows, latb=None):
    B, LT, W2 = u.shape
    W = W2 // 2
    T, Wc = LRU_T, LRU_WC
    n_tot = LT // T
    n_ctx = n_ctx_rows // T
    nj = W // Wc
    halo = 2 * SUBLANES
    hb = T // halo

    def chunk(s):
        if fwd:
            return s
        return jnp.where(s < n_ctx, n_ctx - 1 - s, n_tot + n_ctx - 1 - s)

    in_specs = [
        pl.BlockSpec((B, T, Wc), lambda j, s: (0, chunk(s), j)),
        pl.BlockSpec((B, halo, Wc), lambda j, s: (0, jnp.maximum(chunk(s) * hb - 1, 0), j)),
        pl.BlockSpec((B, halo, Wc),
                     lambda j, s: (0, jnp.minimum((chunk(s) + 1) * hb, LT // halo - 1), j)),
        pl.BlockSpec((4, Wc), lambda j, s: (0, j)),
        pl.BlockSpec((1, Wc), lambda j, s: (0, j)),
        pl.BlockSpec((Wc // LRU_BLOCK, LRU_BLOCK, 2 * LRU_BLOCK), lambda j, s: (j, 0, 0)),
        pl.BlockSpec((2, Wc), lambda j, s: (0, j)),
        pl.BlockSpec((1, Wc), lambda j, s: (0, j)),
    ]
    args = [u, u, u, conv_w, conv_b.reshape(1, W), wg, bg, lam.reshape(1, W)]
    nb = Wc // LRU_BLOCK
    scratch = [pltpu.VMEM((B, Wc), F32), pltpu.VMEM((nb, (T + 3) * B, LRU_BLOCK), F32)]
    if fwd:
        in_specs += [pl.BlockSpec((B, T, Wc), lambda j, s: (0, s, nj + j)),
                     pl.BlockSpec((T, B, Wc), lambda j, s: (s, 0, j))]
        args += [u, latb]
        scratch.append(pltpu.VMEM((nb, T * B, LRU_BLOCK), F32))
        out_specs = pl.BlockSpec((B, T, Wc), lambda j, s: (0, s, j))
        out_shape = jax.ShapeDtypeStruct((B, LT, W), BF16)
    else:
        out_specs = pl.BlockSpec((T, B, Wc), lambda j, s: (chunk(s), 0, j))
        out_shape = jax.ShapeDtypeStruct((LT, B, W), F32)
    return pl.pallas_call(
        functools.partial(_lru_kernel, fwd, T, n_ctx, n_tot),
        grid=(nj, n_tot),
        in_specs=in_specs,
        out_specs=out_specs,
        out_shape=out_shape,
        scratch_shapes=scratch,
        compiler_params=_cparams(("arbitrary", "arbitrary")),
        name="lru_fwd" if fwd else "lru_bwd",
    )(*args)


def _ret_kernel(fwd, C, lg_ref, blk_ref, q_ref, k_ref, v_ref, *rest):
    if fwd:
        gate_ref, ob_ref, o_ref, st_sc, qdec_sc, kdec_sc, mask_sc = rest
    else:
        o_ref, st_sc, qdec_sc, kdec_sc = rest
    s = pl.program_id(1)
    d = 0 if fwd else 1

    @pl.when(s == 0)
    def _():
        st_sc[...] = jnp.zeros_like(st_sc)

    @pl.when(jnp.logical_and(pl.program_id(0) == 0, s == 0))
    def _():
        row_v = lax.broadcasted_iota(jnp.int32, (C, RET_DV), 0).astype(F32)
        row_k = lax.broadcasted_iota(jnp.int32, (C, RET_DK), 0).astype(F32)
        if fwd:
            ri = lax.broadcasted_iota(jnp.int32, (C, C), 0)
            ci = lax.broadcasted_iota(jnp.int32, (C, C), 1)
            diff = (ri - ci).astype(F32)
        for hh in range(RET_HEADS):
            lg = lg_ref[d, hh]
            if fwd:
                qdec_sc[hh] = jnp.exp((row_v + 1.0) * lg)
                kdec_sc[hh] = jnp.exp((C - 1.0 - row_k) * lg)
                mask_sc[hh] = jnp.exp(jnp.where(diff >= 0, diff * lg_ref[0, hh],
                                                -diff * lg_ref[1, hh]))
            else:
                qdec_sc[hh] = jnp.exp((C - row_v) * lg)
                kdec_sc[hh] = jnp.exp(row_k * lg)

    for hh in range(RET_HEADS):
        qh = q_ref[0, :, hh * RET_DK:(hh + 1) * RET_DK]
        kh = k_ref[0, :, hh * RET_DK:(hh + 1) * RET_DK]
        vh = v_ref[0, :, hh * RET_DV:(hh + 1) * RET_DV]
        st = st_sc[hh]
        o = _dot(qh, st.astype(BF16)) * qdec_sc[hh]
        kd = (kh.astype(F32) * kdec_sc[hh]).astype(BF16)
        st_sc[hh] = st * blk_ref[d, hh] + _dot_tn(kd, vh)
        if fwd:
            sc = _dot_nt(qh, kh) * mask_sc[hh]
            o = o + _dot(sc.astype(BF16), vh)
            o = o + ob_ref[0, :, hh * RET_DV:(hh + 1) * RET_DV].astype(F32)
            o = o * lax.rsqrt(jnp.mean(o * o, axis=-1, keepdims=True) + NORM_EPS)
            gate = gate_ref[0, :, hh * RET_DV:(hh + 1) * RET_DV].astype(F32)
            o_ref[0, :, hh * RET_DV:(hh + 1) * RET_DV] = (_silu(gate) * o).astype(o_ref.dtype)
        else:
            o_ref[0, :, hh * RET_DV:(hh + 1) * RET_DV] = o.astype(o_ref.dtype)


def _ret_pass(fwd, u, logg, blk, n_ctx_rows, ob=None):
    B, LT, _ = u.shape
    C = RET_CHUNK
    nk = RET_HEADS * RET_DK
    nv = RET_HEADS * RET_DV
    n_tot = LT // C
    n_ctx = n_ctx_rows // C

    def chunk(s):
        if fwd:
            return s
        return jnp.where(s < n_ctx, n_ctx - 1 - s, n_tot + n_ctx - 1 - s)

    smem = pl.BlockSpec(memory_space=pltpu.SMEM)
    in_specs = [smem, smem,
                pl.BlockSpec((1, C, nk), lambda b, s: (b, chunk(s), 0)),
                pl.BlockSpec((1, C, nk), lambda b, s: (b, chunk(s), 1)),
                pl.BlockSpec((1, C, nv), lambda b, s: (b, chunk(s), 1))]
    args = [logg, blk, u, u, u]
    if fwd:
        in_specs += [pl.BlockSpec((1, C, nv), lambda b, s: (b, s, 2)),
                     pl.BlockSpec((1, C, nv), lambda b, s: (b, s, 0))]
        args += [u, ob]
    return pl.pallas_call(
        functools.partial(_ret_kernel, fwd, C),
        grid=(B, n_tot),
        in_specs=in_specs,
        out_specs=pl.BlockSpec((1, C, nv), lambda b, s: (b, chunk(s), 0)),
        out_shape=jax.ShapeDtypeStruct((B, LT, nv), BF16),
        scratch_shapes=[pltpu.VMEM((RET_HEADS, RET_DK, RET_DV), F32),
                        pltpu.VMEM((RET_HEADS, C, RET_DV), F32),
                        pltpu.VMEM((RET_HEADS, C, RET_DK), F32)]
        + ([pltpu.VMEM((RET_HEADS, C, C), F32)] if fwd else []),
        compiler_params=_cparams(("arbitrary", "arbitrary")),
        name="ret_fwd" if fwd else "ret_bwd",
    )(*args)


def _hy_filter_kernel(z_ref, tt_ref, fw1_ref, fb1_ref, fw2_ref, fb2_ref, fq_ref, w3_ref,
                      delta_ref, o_ref):
    fq = fq_ref[...]
    h1 = jnp.sin(fq * (_dot(z_ref[...], fw1_ref[...], HIGHEST) + fb1_ref[...]))
    h2 = jnp.sin(fq * (_dot(h1, fw2_ref[...], HIGHEST) + fb2_ref[...]))
    kt = _dot_nt(w3_ref[0], h2, HIGHEST)
    for r in range(tt_ref.shape[0]):
        t = tt_ref[r, 0:1, :]
        msk = tt_ref[r, 1:2, :]
        o_ref[r] = kt[:, r * LANES:(r + 1) * LANES] * jnp.exp(-delta_ref[...] * t) * msk


def _hy_circular(z, R):
    Lv = z.shape[0]
    N = R * LANES
    gap = N - 2 * Lv + 1
    zc = jnp.concatenate([z, jnp.zeros((gap, z.shape[1]), z.dtype), jnp.flip(z[1:], axis=0)], axis=0)
    valid = np.concatenate([np.ones(Lv), np.zeros(gap), np.ones(Lv - 1)]).astype(np.float32)
    return zc, valid


def _hy_filter_taps(Lv, R, fw1, fb1, fw2, fb2, fw3, freq):
    W = fw3.shape[1] // 4
    bands = (HY_EMB - 1) // 2
    t = jnp.linspace(0.0, 1.0, Lv, dtype=F32)[:, None]
    w = 2.0 * math.pi * jnp.arange(Lv, dtype=F32)[:, None] / Lv
    fr = jnp.linspace(1e-4, bands - 1, bands, dtype=F32)[None, :]
    z = jnp.concatenate([t, jnp.cos(fr * w), -jnp.sin(fr * w)], axis=-1)
    zc, valid = _hy_circular(z, R)
    zp = jnp.pad(zc, ((0, 0), (0, LANES - HY_EMB)))
    tt = jnp.stack([zc[:, 0].reshape(R, LANES), jnp.asarray(valid).reshape(R, LANES)], axis=1)
    fw1p = jnp.pad(fw1.astype(F32), ((0, LANES - HY_EMB), (0, 0)))
    w3 = fw3.astype(F32).reshape(HY_HIDDEN, 2, 2, W)
    w3d = jnp.stack([w3[:, :, 0, :].reshape(HY_HIDDEN, 2 * W).T,
                     w3[:, :, 1, :].reshape(HY_HIDDEN, 2 * W).T], axis=0)
    deltas = jnp.abs(jnp.linspace(math.log(HY_TARGET) / HY_SLOW_DECAY,
                                  math.log(HY_TARGET) / HY_FAST_DECAY, W, dtype=F32))
    delta2 = jnp.concatenate([deltas, deltas]).reshape(2 * W, 1)
    F2 = 2 * W
    rb = HY_TAP_ROWS
    assert (R // 2) % rb == 0
    half = R // 2 // rb
    kt = pl.pallas_call(
        _hy_filter_kernel,
        grid=(R // rb,),
        in_specs=[pl.BlockSpec((rb * LANES, LANES), lambda r: (r, 0)),
                  pl.BlockSpec((rb, 2, LANES), lambda r: (r, 0, 0)),
                  pl.BlockSpec((LANES, HY_HIDDEN), lambda r: (0, 0)),
                  pl.BlockSpec((1, HY_HIDDEN), lambda r: (0, 0)),
                  pl.BlockSpec((HY_HIDDEN, HY_HIDDEN), lambda r: (0, 0)),
                  pl.BlockSpec((1, HY_HIDDEN), lambda r: (0, 0)),
                  pl.BlockSpec((1, HY_HIDDEN), lambda r: (0, 0)),
                  pl.BlockSpec((1, F2, HY_HIDDEN), lambda r: (jnp.where(r < half, 0, 1), 0, 0)),
                  pl.BlockSpec((F2, 1), lambda r: (0, 0))],
        out_specs=pl.BlockSpec((rb, F2, LANES), lambda r: (r, 0, 0)),
        out_shape=jax.ShapeDtypeStruct((R, F2, LANES), F32),
        compiler_params=_cparams(("arbitrary",)),
        name="hyena_filter_taps",
    )(zp, tt, fw1p, fb1.astype(F32).reshape(1, -1), fw2.astype(F32), fb2.astype(F32).reshape(1, -1),
      freq.astype(F32).reshape(1, -1), w3d, delta2)
    return jnp.transpose(kt, (1, 0, 2))


def _dft_constants(R0, R):
    N = R * LANES
    k1 = np.arange(R)[:, None]
    n1 = np.arange(R0)[None, :]
    f1 = np.exp(-2j * np.pi * k1 * n1 / R)
    ma = np.block([[f1.real, -f1.imag], [f1.imag, f1.real]])
    g1 = np.exp(2j * np.pi * n1.T * k1.T / R) / N
    mai = np.block([[g1.real, -g1.imag], [g1.imag, g1.real]])
    n2 = np.arange(LANES)
    tw = np.exp(-2j * np.pi * np.arange(R)[:, None] * n2[None, :] / N)
    f2 = np.exp(-2j * np.pi * n2[:, None] * n2[None, :] / LANES)
    mc = np.block([[f2.real, f2.imag], [-f2.imag, f2.real]])
    mci = np.block([[f2.real, -f2.imag], [f2.imag, f2.real]])
    f1_full = np.exp(-2j * np.pi * k1 * np.arange(R)[None, :] / R)
    as_bf = lambda m: jnp.asarray(m, F32).astype(BF16)
    return dict(ma=as_bf(ma), mai=as_bf(mai), mc=as_bf(mc), mci=as_bf(mci),
                tr=jnp.asarray(tw.real, F32), ti=jnp.asarray(tw.imag, F32),
                ma_real=as_bf(np.concatenate([f1_full.real, f1_full.imag], axis=0)))


def _hy_fft_kernel(R, cb, k_ref, ma_ref, tr_ref, ti_ref, mc_ref, or_ref, oi_ref):
    cols = [k_ref[c] for c in range(cb)]
    inv = []
    for col in cols:
        sabs = jnp.sum(jnp.sum(jnp.abs(col), axis=1, keepdims=True), axis=0, keepdims=True)
        inv.append(1.0 / sabs)
    rhs = jnp.concatenate(cols, axis=1)
    hi = rhs.astype(BF16)
    lo = (rhs - hi.astype(F32)).astype(BF16)
    a = _dot(ma_ref[...], hi) + _dot(ma_ref[...], lo)
    tr = tr_ref[...]
    ti = ti_ref[...]
    rows = []
    for c in range(cb):
        ar = a[:R, c * LANES:(c + 1) * LANES]
        ai = a[R:, c * LANES:(c + 1) * LANES]
        rows.append(jnp.concatenate([ar * tr - ai * ti, ar * ti + ai * tr], axis=1))
    lhs = jnp.concatenate(rows, axis=0)
    hi = lhs.astype(BF16)
    lo = (lhs - hi.astype(F32)).astype(BF16)
    x = _dot(hi, mc_ref[...]) + _dot(lo, mc_ref[...])
    for c in range(cb):
        or_ref[c] = (x[c * R:(c + 1) * R, :LANES] * inv[c]).astype(or_ref.dtype)
        oi_ref[c] = (x[c * R:(c + 1) * R, LANES:] * inv[c]).astype(oi_ref.dtype)


def _hy_filter_spectrum(kt, R, consts):
    F2 = kt.shape[0]
    cb = HY_CB_SHORT
    spec = pl.BlockSpec((cb, R, LANES), lambda j: (j, 0, 0))
    full = lambda a: pl.BlockSpec(a.shape, lambda j: (0,) * a.ndim)
    return pl.pallas_call(
        functools.partial(_hy_fft_kernel, R, cb),
        grid=(F2 // cb,),
        in_specs=[spec, full(consts["ma_real"]), full(consts["tr"]), full(consts["ti"]),
                  full(consts["mc"])],
        out_specs=[spec, spec],
        out_shape=[jax.ShapeDtypeStruct((F2, R, LANES),
                                        BF16 if R % (2 * SUBLANES) == 0 else F32)] * 2,
        compiler_params=_cparams(("arbitrary",)),
        name="hyena_filter_fft",
    )(kt, consts["ma_real"], consts["tr"], consts["ti"], consts["mc"])


def _hy_conv_kernel(R0, R, Lv, cb, W, cw_ref, cbias_ref, skip_ref, v_ref, x1_ref, x2_ref, g_ref,
                    k0r_ref, k0i_ref, k1r_ref, k1i_ref, ma_ref, mai_ref, mc_ref, mci_ref,
                    tr_ref, ti_ref, o_ref):
    j = pl.program_id(0)
    row = lax.broadcasted_iota(jnp.int32, (R0, LANES), 0)
    lane = lax.broadcasted_iota(jnp.int32, (R0, LANES), 1)
    pos = row * LANES + lane
    cdt = BF16 if R % (2 * SUBLANES) == 0 else F32
    tr = tr_ref[...].astype(cdt)
    ti = ti_ref[...].astype(cdt)

    def cmul(xr, xi, yr, yi):
        return xr * yr - xi * yi, xr * yi + xi * yr

    def short_conv(ref, part, c, grp):
        x = ref[0, part, c].astype(F32)
        ch = grp * W + j * cb + c
        r = pltpu.roll(x, 1, 1)
        prev = jnp.where(lane == 0, pltpu.roll(r, 1, 0), r)
        prev = jnp.where(pos == 0, 0.0, prev)
        r = pltpu.roll(x, LANES - 1, 1)
        nxt = jnp.where(lane == LANES - 1, pltpu.roll(r, R0 - 1, 0), r)
        nxt = jnp.where(pos == R0 * LANES - 1, 0.0, nxt)
        y = cw_ref[0, ch] * prev + cw_ref[1, ch] * x + cw_ref[2, ch] * nxt + cbias_ref[ch]
        if Lv < R0 * LANES:
            y = jnp.where(pos < Lv, y, 0.0)
        return y

    def long_conv(re, im, kr_ref, ki_ref):
        rhs = jnp.concatenate([jnp.concatenate([re[c], im[c]], axis=0) for c in range(cb)], axis=1)
        a = _dot(ma_ref[...], rhs.astype(BF16))
        rows = []
        for c in range(cb):
            ar = a[:R, c * LANES:(c + 1) * LANES].astype(cdt)
            ai = a[R:, c * LANES:(c + 1) * LANES].astype(cdt)
            rows.append(jnp.concatenate(cmul(ar, ai, tr, ti), axis=1))
        x = _dot(jnp.concatenate(rows, axis=0).astype(BF16), mc_ref[...])
        rows = []
        for c in range(cb):
            xr = x[c * R:(c + 1) * R, :LANES].astype(cdt)
            xi = x[c * R:(c + 1) * R, LANES:].astype(cdt)
            rows.append(jnp.concatenate(
                cmul(xr, xi, kr_ref[c].astype(cdt), ki_ref[c].astype(cdt)), axis=1))
        b = _dot(jnp.concatenate(rows, axis=0).astype(BF16), mci_ref[...])
        cols = []
        for c in range(cb):
            br = b[c * R:(c + 1) * R, :LANES].astype(cdt)
            bi = b[c * R:(c + 1) * R, LANES:].astype(cdt)
            cols.append(jnp.concatenate(cmul(br, bi, tr, -ti), axis=0))
        y = _dot(mai_ref[...], jnp.concatenate(cols, axis=1).astype(BF16))
        out_re = [y[:R0, c * LANES:(c + 1) * LANES] for c in range(cb)]
        out_im = [y[R0:, c * LANES:(c + 1) * LANES] for c in range(cb)]
        return out_re, out_im

    v = [[short_conv(v_ref, p, c, 0) for c in range(cb)] for p in range(2)]
    x1 = [[short_conv(x1_ref, p, c, 1) for c in range(cb)] for p in range(2)]
    cr, ci = long_conv(v[0], v[1], k0r_ref, k0i_ref)
    conv = (cr, ci)
    y = [[x1[p][c] * (conv[p][c] + v[p][c] * skip_ref[0, j * cb + c]) for c in range(cb)]
         for p in range(2)]
    x2 = [[short_conv(x2_ref, p, c, 2) for c in range(cb)] for p in range(2)]
    cr, ci = long_conv(y[0], y[1], k1r_ref, k1i_ref)
    conv = (cr, ci)
    for p in range(2):
        for c in range(cb):
            y2 = x2[p][c] * (conv[p][c] + y[p][c] * skip_ref[1, j * cb + c])
            gate = g_ref[0, p, c].astype(F32)
            o_ref[0, p, c] = (y2 * _silu(gate)).astype(o_ref.dtype)


def _hy_conv(ut, kfr, kfi, conv_w, conv_b, skip, Lv, R0, R, consts):
    G, _, W4, _, _ = ut.shape
    W = W4 // 4
    cb = HY_CB if R0 * LANES >= HY_SHORT_SEQ else HY_CB_SHORT
    nj = W // cb
    smem = pl.BlockSpec(memory_space=pltpu.SMEM)
    full = lambda a: pl.BlockSpec(a.shape, lambda j, g: (0,) * a.ndim)

    def data(grp):
        return pl.BlockSpec((1, 2, cb, R0, LANES), lambda j, g: (g, 0, grp * nj + j, 0, 0))

    def filt(order):
        return pl.BlockSpec((cb, R, LANES), lambda j, g: (order * nj + j, 0, 0))

    return pl.pallas_call(
        functools.partial(_hy_conv_kernel, R0, R, Lv, cb, W),
        grid=(nj, G),
        in_specs=[smem, smem, smem, data(0), data(1), data(2), data(3),
                  filt(0), filt(0), filt(1), filt(1),
                  full(consts["ma"]), full(consts["mai"]), full(consts["mc"]), full(consts["mci"]),
                  full(consts["tr"]), full(consts["ti"])],
        out_specs=pl.BlockSpec((1, 2, cb, R0, LANES), lambda j, g: (g, 0, j, 0, 0)),
        out_shape=jax.ShapeDtypeStruct((G, 2, W, R0, LANES), BF16),
        compiler_params=_cparams(("arbitrary", "arbitrary")),
        name="hyena_conv",
    )(conv_w, conv_b, skip, ut, ut, ut, ut, kfr, kfi, kfr, kfi,
      consts["ma"], consts["mai"], consts["mc"], consts["mci"], consts["tr"], consts["ti"])


def _hy_segment(ut, rows, conv_w, conv_b, fw1, fb1, fw2, fb2, fw3, freq, skip):
    B, W4, rows_in, _ = ut.shape
    W = W4 // 4
    Lv = rows * LANES
    R0 = -(-rows // SUBLANES) * SUBLANES
    R = max(R0, -(-(2 * rows) // SUBLANES) * SUBLANES)
    consts = _dft_constants(R0, R)
    kt = _hy_filter_taps(Lv, R, fw1, fb1, fw2, fb2, fw3, freq)
    kfr, kfi = _hy_filter_spectrum(kt, R, consts)
    if R0 > rows_in:
        ut = jnp.pad(ut, ((0, 0), (0, 0), (0, R0 - rows_in), (0, 0)))
    else:
        assert R0 == rows
    ut = ut.reshape(B // 2, 2, W4, ut.shape[2], LANES)
    z = _hy_conv(ut, kfr, kfi, conv_w.astype(F32), conv_b.astype(F32), skip.astype(F32),
                 Lv, R0, R, consts)
    z = z.reshape(B, W, R0, LANES)[:, :, :rows]
    return jnp.transpose(z, (0, 2, 3, 1)).reshape(B, Lv, W)


def _attn_kernel(n_chunks, ck, q0_ref, q1_ref, q2_ref, k_ref, v_ref, g_ref, o_ref,
                 sa_sc, sb_sc, ma_sc):
    R = ATT_HEADS // ATT_KV
    tq = q1_ref.shape[1]

    def stack_heads(q_ref):
        return jnp.concatenate([q_ref[0, :, r * ATT_D:(r + 1) * ATT_D] for r in range(R)], axis=0)

    ncb = ck // LANES

    def scores(q, s_sc, c):
        s = _dot_nt(q, k_ref[0, c * ck:(c + 1) * ck, :])
        s_sc[:, c * ck:(c + 1) * ck] = s
        mp = s[:, :LANES]
        for j in range(1, ncb):
            mp = jnp.maximum(mp, s[:, j * LANES:(j + 1) * LANES])
        return mp

    def weighted(s_sc, m_rep, c):
        ps = [jnp.exp2(s_sc[:, c * ck + j * LANES:c * ck + (j + 1) * LANES] - m_rep)
              for j in range(ncb)]
        lp = ps[0]
        for j in range(1, ncb):
            lp = lp + ps[j]
        p = jnp.concatenate([pj.astype(BF16) for pj in ps], axis=1)
        return lp, _dot(p, v_ref[0, c * ck:(c + 1) * ck, :])

    def lane_max_rep(mp):
        return jnp.broadcast_to(jnp.max(mp, axis=-1, keepdims=True), mp.shape)

    def finish(acc, lp, half):
        o = acc / jnp.sum(lp, axis=-1, keepdims=True)
        for r in range(R):
            rows = slice(half * tq, (half + 1) * tq)
            gate = g_ref[0, rows, r * ATT_D:(r + 1) * ATT_D].astype(F32)
            o_ref[0, rows, r * ATT_D:(r + 1) * ATT_D] = (
                _silu(gate) * o[r * tq:(r + 1) * tq]).astype(o_ref.dtype)

    def phase(s_cur, m_cur, q_next, s_next):
        m_next = l = acc = None
        for c in range(n_chunks):
            mc = scores(q_next, s_next, c)
            lc, pv = weighted(s_cur, m_cur, c)
            m_next = mc if m_next is None else jnp.maximum(m_next, mc)
            l = lc if l is None else l + lc
            acc = pv if acc is None else acc + pv
        return acc, l, lane_max_rep(m_next)

    @pl.when(pl.program_id(2) == 0)
    def _():
        q0 = stack_heads(q0_ref)
        m = None
        for c in range(n_chunks):
            mc = scores(q0, sa_sc, c)
            m = mc if m is None else jnp.maximum(m, mc)
        ma_sc[...] = lane_max_rep(m)

    acc, l, mb = phase(sa_sc, ma_sc[...], stack_heads(q1_ref), sb_sc)
    finish(acc, l, 0)
    acc, l, ma = phase(sb_sc, mb, stack_heads(q2_ref), sa_sc)
    finish(acc, l, 1)
    ma_sc[...] = ma


def _attention(u, n_ctx_rows):
    B, LT, _ = u.shape
    L = LT - n_ctx_rows
    R = ATT_HEADS // ATT_KV
    tq = ATT_TQ
    ck = max(t for t in range(LANES, ATT_CHUNK_MAX + 1, LANES) if LT % t == 0)
    qw = R * ATT_D
    n_tiles = L // tq
    assert n_tiles % 2 == 0 and n_ctx_rows % (2 * tq) == 0
    q_off = n_ctx_rows // tq
    k_col = ATT_HEADS
    v_col = ATT_HEADS + ATT_KV
    g_col = (ATT_HEADS + 2 * ATT_KV) * ATT_D // qw

    def q_spec(tile):
        return pl.BlockSpec((1, tq, qw), lambda b, g, i: (b, q_off + tile(i), g))

    return pl.pallas_call(
        functools.partial(_attn_kernel, LT // ck, ck),
        grid=(B, ATT_KV, n_tiles // 2),
        in_specs=[q_spec(lambda i: 0),
                  q_spec(lambda i: 2 * i + 1),
                  q_spec(lambda i: jnp.minimum(2 * i + 2, n_tiles - 1)),
                  pl.BlockSpec((1, LT, ATT_D), lambda b, g, i: (b, 0, k_col + g),
                               pipeline_mode=pl.Buffered(1)),
                  pl.BlockSpec((1, LT, ATT_D), lambda b, g, i: (b, 0, v_col + g),
                               pipeline_mode=pl.Buffered(1)),
                  pl.BlockSpec((1, 2 * tq, qw), lambda b, g, i: (b, i + q_off // 2, g_col + g))],
        out_specs=pl.BlockSpec((1, 2 * tq, qw), lambda b, g, i: (b, i, g)),
        out_shape=jax.ShapeDtypeStruct((B, L, ATT_HEADS * ATT_D), BF16),
        scratch_shapes=[pltpu.VMEM((R * tq, LT), F32), pltpu.VMEM((R * tq, LT), F32),
                        pltpu.VMEM((R * tq, LANES), F32)],
        compiler_params=_cparams(("arbitrary", "arbitrary", "arbitrary")),
        name="attention",
    )(u, u, u, u, u, u)


def _rope_angles(pos, dim):
    half = dim // 2
    inv = ROPE_THETA ** (-jnp.arange(half, dtype=F32) / half)
    return pos[:, None] * inv[None, :]


def _with_ctx_identity(cos_like, sin_like, n_ctx_rows):
    ones = jnp.ones((n_ctx_rows, cos_like.shape[1]), F32)
    return (jnp.concatenate([ones, cos_like], axis=0),
            jnp.concatenate([jnp.zeros_like(ones), sin_like], axis=0))


def _layer_lru(xs, Lc, ms, norm_g, w_in, conv_w, conv_b, w_r, b_r, w_i, b_i, lam, w_out):
    n_ctx_blk = Lc // TOK_TILE
    u = _norm_proj(xs, ms, norm_g, w_in.astype(BF16), n_ctx_blk)
    lat = None
    for d in (1, 0):
        wg = (0.5 * jnp.concatenate([w_r[d], w_i[d]], axis=-1)).astype(BF16)
        bg = 0.5 * jnp.stack([b_r[d], b_i[d]], axis=0).astype(F32)
        lat = _lru_pass(d == 0, u, conv_w.astype(F32), conv_b.astype(F32), wg, bg,
                        lam[d].astype(F32), Lc, latb=lat)
    return _out_proj_residual(lat, w_out.astype(BF16), xs, ms, n_ctx_blk)


def _layer_ret(xs, Lc, ms, norm_g, w_in, decay_logit, w_out):
    n_ctx_blk = Lc // TOK_TILE
    L = xs.shape[1] - Lc
    ang = _rope_angles(jnp.arange(L, dtype=F32), RET_DK)
    cos, sin = _with_ctx_identity(jnp.cos(ang), jnp.sin(ang), Lc)
    u = _norm_proj(xs, ms, norm_g, w_in.astype(BF16), n_ctx_blk, "ret", (cos, sin))
    dl = decay_logit.astype(F32)
    logg = -(jnp.maximum(-dl, 0.0) + jnp.log1p(jnp.exp(-jnp.abs(dl))))
    blk = jnp.exp(RET_CHUNK * logg)
    ob = _ret_pass(False, u, logg, blk, Lc)
    z = _ret_pass(True, u, logg, blk, Lc, ob=ob)
    return _out_proj_residual(z, w_out.astype(BF16), xs, ms, n_ctx_blk)


def _layer_hyena(xs, Lc, ms, norm_g, w_in, conv_w, conv_b, fw1, fb1, fw2, fb2, fw3, freq, skip,
                 w_out):
    n_ctx_blk = Lc // TOK_TILE
    L = xs.shape[1] - Lc
    u = _norm_proj(xs, ms, norm_g, w_in.astype(BF16), n_ctx_blk, ctx_last=True)
    hy_args = (conv_w, conv_b, fw1, fb1, fw2, fb2, fw3, freq, skip)
    assert Lc % LANES == 0 and L % LANES == 0
    u4 = u.reshape(u.shape[0], (L + Lc) // LANES, LANES, u.shape[2])
    ut = jnp.transpose(u4, (0, 3, 1, 2))
    z = jnp.concatenate([_hy_segment(ut[:, :, L // LANES:], Lc // LANES, *hy_args),
                         _hy_segment(ut, L // LANES, *hy_args)], axis=1)
    return _out_proj_residual(z, w_out.astype(BF16), xs, ms, n_ctx_blk)


def _layer_attn(xs, Lc, ms, norm_g, w_in, q_norm_g, k_norm_g, w_out, final_g=None):
    n_ctx_blk = Lc // TOK_TILE
    L = xs.shape[1] - Lc
    rows = L // GRID_W
    row = jnp.repeat(jnp.arange(rows, dtype=F32), GRID_W)
    col = jnp.tile(jnp.arange(GRID_W, dtype=F32), rows)
    ang = jnp.concatenate([_rope_angles(row, ATT_D // 2), _rope_angles(col, ATT_D // 2)], axis=-1)
    cos, sin = _with_ctx_identity(jnp.cos(ang), jnp.sin(ang), Lc)
    c2 = jnp.concatenate([cos, cos], axis=-1)
    s2 = jnp.concatenate([-sin, sin], axis=-1)
    u = _norm_proj(xs, ms, norm_g, w_in.astype(BF16), n_ctx_blk, "att",
                   (c2, s2, q_norm_g.astype(F32).reshape(1, ATT_D),
                    k_norm_g.astype(F32).reshape(1, ATT_D)))
    z = _attention(u, Lc)
    return _out_proj_residual(z, w_out.astype(BF16), xs, ms, n_ctx_blk, x_off_blk=n_ctx_blk,
                              final_g=final_g)


def kernel(x, c, ctx, c_ctx, lru_mod_w, lru_mod_b, lru_norm_g, lru_w_in, lru_conv_w, lru_conv_b, lru_w_r, lru_b_r, lru_w_i, lru_b_i, lru_lambda, lru_w_out, ret_mod_w, ret_mod_b, ret_norm_g, ret_w_in, ret_decay_logit, ret_w_out, hy_mod_w, hy_mod_b, hy_norm_g, hy_w_in, hy_conv_w, hy_conv_b, hy_fw1, hy_fb1, hy_fw2, hy_fb2, hy_fw3, hy_freq, hy_skip, hy_w_out, att_mod_w, att_mod_b, att_norm_g, att_w_in, att_q_norm_g, att_k_norm_g, att_w_out, final_norm_g):
    B, L, D = x.shape
    Lc = ctx.shape[1]
    LT = Lc + L
    assert B == SUBLANES and B % 2 == 0
    assert Lc % TOK_TILE == 0 and L % TOK_TILE == 0 and Lc % RET_CHUNK == 0 and Lc % LRU_T == 0
    n_ctx_blk = Lc // TOK_TILE

    cvec = jnp.zeros((16, D), F32).at[:B].set(c).at[B].set(c_ctx)

    def modsel(mod_w, mod_b):
        m = _modulation(cvec, mod_w, mod_b).reshape(16, 3, D)
        return jnp.stack([jnp.broadcast_to(m[B][None], (B, 3, D)), m[:B]], axis=0)

    xs = jnp.concatenate([ctx, x], axis=1)
    xs = _layer_lru(xs, Lc, modsel(lru_mod_w, lru_mod_b), lru_norm_g, lru_w_in, lru_conv_w,
                    lru_conv_b, lru_w_r, lru_b_r, lru_w_i, lru_b_i, lru_lambda, lru_w_out)
    xs = _layer_ret(xs, Lc, modsel(ret_mod_w, ret_mod_b), ret_norm_g, ret_w_in, ret_decay_logit,
                    ret_w_out)
    xs = _layer_hyena(xs, Lc, modsel(hy_mod_w, hy_mod_b), hy_norm_g, hy_w_in, hy_conv_w, hy_conv_b,
                      hy_fw1, hy_fb1, hy_fw2, hy_fb2, hy_fw3, hy_freq, hy_skip, hy_w_out)
    return _layer_attn(xs, Lc, modsel(att_mod_w, att_mod_b), att_norm_g, att_w_in, att_q_norm_g,
                       att_k_norm_g, att_w_out, final_g=final_norm_g)
```

```python
import functools
import math

import numpy as np
import jax
import jax.numpy as jnp
from jax import lax
from jax.experimental import pallas as pl
from jax.experimental.pallas import tpu as pltpu

F32 = jnp.float32
BF16 = jnp.bfloat16
HIGHEST = lax.Precision.HIGHEST

NORM_EPS = 1e-6
ROPE_THETA = 10000.0
GRID_W = 64
LANES = 128
SUBLANES = 8
VMEM_LIMIT = 59 * 1024 * 1024

TOK_TILE = 256

LRU_C = 8.0
LRU_BLOCK = 128
LRU_T = 128
LRU_WC = 512

RET_HEADS = 4
RET_DK = 256
RET_DV = 512
RET_CHUNK = 256

HY_EMB = 33
HY_HIDDEN = 64
HY_FAST_DECAY = 0.3
HY_SLOW_DECAY = 1.5
HY_TARGET = 1e-2
HY_CB = 8
HY_CB_SHORT = 32
HY_SHORT_SEQ = 4096
HY_TAP_ROWS = 4

ATT_HEADS = 8
ATT_KV = 2
ATT_D = 128
ATT_TQ = 128
ATT_CHUNK_MAX = 256


def _cparams(sem):
    return pltpu.CompilerParams(dimension_semantics=sem, vmem_limit_bytes=VMEM_LIMIT)


def _dot(a, b, precision=None):
    return lax.dot_general(a, b, (((1,), (0,)), ((), ())), precision=precision,
                           preferred_element_type=F32)


def _dot_nt(a, b, precision=None):
    return lax.dot_general(a, b, (((1,), (1,)), ((), ())), precision=precision,
                           preferred_element_type=F32)


def _dot_tn(a, b):
    return lax.dot_general(a, b, (((0,), (0,)), ((), ())), preferred_element_type=F32)


def _silu(x):
    xh = 0.5 * x
    return xh * jnp.tanh(xh) + xh


def _mod_kernel(c_ref, w_ref, b_ref, o_ref):
    o_ref[...] = _dot(_silu(c_ref[...]), w_ref[...], HIGHEST) + b_ref[...]


def _modulation(cvec, mod_w, mod_b):
    D = cvec.shape[1]
    return pl.pallas_call(
        _mod_kernel,
        grid=(3,),
        in_specs=[pl.BlockSpec((16, D), lambda n: (0, 0)),
                  pl.BlockSpec((D, D), lambda n: (0, n)),
                  pl.BlockSpec((1, D), lambda n: (0, n))],
        out_specs=pl.BlockSpec((16, D), lambda n: (0, n)),
        out_shape=jax.ShapeDtypeStruct((16, 3 * D), F32),
        compiler_params=_cparams(("arbitrary",)),
        name="modulation",
    )(cvec, mod_w, mod_b.reshape(1, 3 * D))


def _stream_rows(n_split, refs):
    if n_split:
        xc_ref, x_ref, *refs = refs
        return jnp.where(pl.program_id(1) < n_split, xc_ref[0], x_ref[0]), refs
    x_ref, *refs = refs
    return x_ref[0], refs


def _stream_specs(x, T, n_ctx_blk, off=0):
    if isinstance(x, tuple):
        ctx, lat = x
        D = lat.shape[2]
        return ([pl.BlockSpec((1, T, D), lambda b, i: (b, jnp.minimum(i, n_ctx_blk - 1), 0)),
                 pl.BlockSpec((1, T, D), lambda b, i: (b, jnp.maximum(i - n_ctx_blk, 0), 0))],
                [ctx, lat], n_ctx_blk)
    D = x.shape[2]
    return [pl.BlockSpec((1, T, D), lambda b, i: (b, i + off, 0))], [x], 0


def _nmm_kernel(mode, n_split, *refs):
    x, (mod_ref, g_ref, w_ref, *rest) = _stream_rows(n_split, refs)
    o_ref = rest[-1]
    ms = jnp.mean(x * x, axis=-1, keepdims=True)
    xn = x * lax.rsqrt(ms + NORM_EPS) * g_ref[...]
    mod = mod_ref[0, 0]
    h = xn * (1.0 + mod[1:2]) + mod[0:1]
    u = _dot(h.astype(BF16), w_ref[...])
    if mode == "plain":
        o_ref[0] = u.astype(o_ref.dtype)
    elif mode == "ret":
        cos = rest[0][...]
        sin = rest[1][...]
        nk = RET_HEADS * RET_DK
        half = RET_DK // 2
        for base, scale in ((0, 1.0), (nk, RET_DK ** -0.5)):
            for hh in range(RET_HEADS):
                c0 = base + hh * RET_DK
                x1 = u[:, c0:c0 + half]
                x2 = u[:, c0 + half:c0 + RET_DK]
                o_ref[0, :, c0:c0 + half] = ((x1 * cos - x2 * sin) * scale).astype(o_ref.dtype)
                o_ref[0, :, c0 + half:c0 + RET_DK] = ((x1 * sin + x2 * cos) * scale).astype(o_ref.dtype)
        o_ref[0, :, 2 * nk:] = u[:, 2 * nk:].astype(o_ref.dtype)
    elif mode == "att":
        c2 = rest[0][...]
        s2 = rest[1][...]
        qg = rest[2][...]
        kg = rest[3][...]
        nqk = ATT_HEADS + ATT_KV
        for hh in range(nqk):
            xh = u[:, hh * ATT_D:(hh + 1) * ATT_D]
            g = qg if hh < ATT_HEADS else kg
            r = lax.rsqrt(jnp.mean(xh * xh, axis=-1, keepdims=True) + NORM_EPS)
            xh = xh * r * g
            rot = xh * c2 + pltpu.roll(xh, ATT_D // 2, 1) * s2
            if hh < ATT_HEADS:
                rot = rot * (ATT_D ** -0.5 * math.log2(math.e))
            o_ref[0, :, hh * ATT_D:(hh + 1) * ATT_D] = rot.astype(o_ref.dtype)
        o_ref[0, :, nqk * ATT_D:] = u[:, nqk * ATT_D:].astype(o_ref.dtype)
    else:
        raise ValueError(mode)


def _norm_proj(x, modsel, norm_g, w_bf16, n_ctx_blk, mode="plain", extras=(), ctx_last=False):
    T = TOK_TILE
    x_specs, x_args, n_split = _stream_specs(x, T, n_ctx_blk)
    B, _, D = x_args[-1].shape
    LT = sum(a.shape[1] for a in x_args)
    F = w_bf16.shape[1]
    n_blk = LT // T

    def out_blk(i):
        if not ctx_last:
            return i
        return jnp.where(i < n_ctx_blk, i + n_blk - n_ctx_blk, i - n_ctx_blk)

    tab_specs = []
    for e in extras:
        if e.shape[0] == 1:
            tab_specs.append(pl.BlockSpec(e.shape, lambda b, i: (0, 0)))
        else:
            tab_specs.append(pl.BlockSpec((T, e.shape[1]), lambda b, i: (i, 0)))
    return pl.pallas_call(
        functools.partial(_nmm_kernel, mode, n_split),
        grid=(B, LT // T),
        in_specs=x_specs + [
                  pl.BlockSpec((1, 1, 3, D), lambda b, i: (jnp.where(i < n_ctx_blk, 0, 1), b, 0, 0)),
                  pl.BlockSpec((1, D), lambda b, i: (0, 0)),
                  pl.BlockSpec((D, F), lambda b, i: (0, 0), pipeline_mode=pl.Buffered(1)),
                  ] + tab_specs,
        out_specs=pl.BlockSpec((1, T, F), lambda b, i: (b, out_blk(i), 0)),
        out_shape=jax.ShapeDtypeStruct((B, LT, F), BF16),
        compiler_params=_cparams(("arbitrary", "arbitrary")),
        name="norm_proj_" + mode,
    )(*x_args, modsel, norm_g.reshape(1, D), w_bf16, *extras)


def _opr_kernel(n_split, z_ref, w_ref, *refs):
    x, (mod_ref, *rest) = _stream_rows(n_split, refs)
    o_ref = rest[-1]
    y = _dot(z_ref[0], w_ref[...])
    x = x + mod_ref[0, 0][2:3] * y
    if len(rest) == 2:
        ms = jnp.mean(x * x, axis=-1, keepdims=True)
        x = x * lax.rsqrt(ms + NORM_EPS) * rest[0][...]
    o_ref[0] = x


def _out_proj_residual(z, w_bf16, x, modsel, n_ctx_blk, x_off_blk=0, final_g=None):
    B, Lz, Fz = z.shape
    T = TOK_TILE
    x_specs, x_args, n_split = _stream_specs(x, T, n_ctx_blk, x_off_blk)
    D = x_args[-1].shape[2]
    in_specs = [pl.BlockSpec((1, T, Fz), lambda b, i: (b, i, 0)),
                pl.BlockSpec((Fz, D), lambda b, i: (0, 0), pipeline_mode=pl.Buffered(1))
                ] + x_specs + [
                pl.BlockSpec((1, 1, 3, D),
                             lambda b, i: (jnp.where(i + x_off_blk < n_ctx_blk, 0, 1), b, 0, 0))]
    args = [z, w_bf16, *x_args, modsel]
    if final_g is not None:
        in_specs.append(pl.BlockSpec((1, D), lambda b, i: (0, 0)))
        args.append(final_g.astype(F32).reshape(1, D))
    return pl.pallas_call(
        functools.partial(_opr_kernel, n_split),
        grid=(B, Lz // T),
        in_specs=in_specs,
        out_specs=pl.BlockSpec((1, T, D), lambda b, i: (b, i, 0)),
        out_shape=jax.ShapeDtypeStruct((B, Lz, D), F32),
        compiler_params=_cparams(("arbitrary", "arbitrary")),
        name="out_proj_residual",
    )(*args)


def _lru_kernel(fwd, T, n_ctx, n_tot, u_ref, up_ref, un_ref, cw_ref, cb_ref, wg_ref, bg_ref,
                lam_ref, *rest):
    if fwd:
        gate_ref, latb_ref, o_ref, h_sc, ext_sc, hs_sc = rest
    else:
        o_ref, h_sc, ext_sc = rest
    Wc = u_ref.shape[2]
    NB = SUBLANES
    HALO = up_ref.shape[1]
    s = pl.program_id(1)
    if fwd:
        c = s
    else:
        c = jnp.where(s < n_ctx, n_ctx - 1 - s, n_tot + n_ctx - 1 - s)

    @pl.when(s == 0)
    def _():
        h_sc[...] = jnp.zeros_like(h_sc)

    first = jnp.logical_or(c == 0, c == n_ctx)
    last = jnp.logical_or(c == n_ctx - 1, c == n_tot - 1)
    pm = jnp.where(first, 0.0, 1.0)
    nm = jnp.where(last, 0.0, 1.0)
    for b in range(NB):
        prev = up_ref[b, HALO - 2:HALO, :].astype(F32) * pm
        nxt = un_ref[b, 0:1, :].astype(F32) * nm
        for n in range(Wc // LRU_BLOCK):
            sl = slice(n * LRU_BLOCK, (n + 1) * LRU_BLOCK)
            ext_sc[n, pl.ds(b, 1), :] = prev[0:1, sl]
            ext_sc[n, pl.ds(NB + b, 1), :] = prev[1:2, sl]
            ext_sc[n, pl.ds(2 * NB + b, T, stride=NB), :] = u_ref[b, :, sl].astype(F32)
            ext_sc[n, pl.ds((T + 2) * NB + b, 1), :] = nxt[:, sl]
    lam = lam_ref[...]
    sp = jnp.maximum(-lam, 0.0) + jnp.log1p(jnp.exp(-jnp.abs(lam)))
    c2 = (-0.5 * LRU_C * math.log2(math.e)) * sp
    for n in range(Wc // LRU_BLOCK):
        sl = slice(n * LRU_BLOCK, (n + 1) * LRU_BLOCK)
        xn = cb_ref[:, sl] + cw_ref[0:1, sl] * ext_sc[n, 0:T * NB, :]
        for k in range(1, 4):
            xn = xn + cw_ref[k:k + 1, sl] * ext_sc[n, k * NB:(k + T) * NB, :]
        g = _dot(xn.astype(BF16), wg_ref[n])
        tr = jnp.tanh(g[:, :LRU_BLOCK] + bg_ref[0:1, sl])
        ti = jnp.tanh(g[:, LRU_BLOCK:] + bg_ref[1:2, sl])
        a = jnp.exp2(c2[:, sl] * tr + c2[:, sl])
        y = 1.0 - a * a
        xh = 0.5 * xn
        bb = jnp.where(y > 0.0, y * lax.rsqrt(y), 0.0) * (xh * ti + xh)
        a3 = a.reshape(T, SUBLANES, LRU_BLOCK)
        b3 = bb.reshape(T, SUBLANES, LRU_BLOCK)
        h = h_sc[:, sl]
        for t in (range(T) if fwd else range(T - 1, -1, -1)):
            h = a3[t] * h + b3[t]
            if fwd:
                hs_sc[n, t * NB:(t + 1) * NB, :] = h + latb_ref[t, :, sl]
            else:
                o_ref[t, :, sl] = h
        h_sc[:, sl] = h
        if fwd:
            for b in range(NB):
                hb = hs_sc[n, pl.ds(b, T, stride=NB), :]
                o_ref[b, :, sl] = (hb * _silu(gate_ref[b, :, sl].astype(F32))).astype(o_ref.dtype)


def _lru_pass(fwd, u, conv_w, conv_b, wg, bg, lam, n_ctx_rows, latb=None):
    B, LT, W2 = u.shape
    W = W2 // 2
    T, Wc = LRU_T, LRU_WC
    n_tot = LT // T
    n_ctx = n_ctx_rows // T
    nj = W // Wc
    halo = 2 * SUBLANES
    hb = T // halo

    def chunk(s):
        if fwd:
            return s
        return jnp.where(s < n_ctx, n_ctx - 1 - s, n_tot + n_ctx - 1 - s)

    in_specs = [
        pl.BlockSpec((B, T, Wc), lambda j, s: (0, chunk(s), j)),
        pl.BlockSpec((B, halo, Wc), lambda j, s: (0, jnp.maximum(chunk(s) * hb - 1, 0), j)),
        pl.BlockSpec((B, halo, Wc),
                     lambda j, s: (0, jnp.minimum((chunk(s) + 1) * hb, LT // halo - 1), j)),
        pl.BlockSpec((4, Wc), lambda j, s: (0, j)),
        pl.BlockSpec((1, Wc), lambda j, s: (0, j)),
        pl.BlockSpec((Wc // LRU_BLOCK, LRU_BLOCK, 2 * LRU_BLOCK), lambda j, s: (j, 0, 0)),
        pl.BlockSpec((2, Wc), lambda j, s: (0, j)),
        pl.BlockSpec((1, Wc), lambda j, s: (0, j)),
    ]
    args = [u, u, u, conv_w, conv_b.reshape(1, W), wg, bg, lam.reshape(1, W)]
    nb = Wc // LRU_BLOCK
    scratch = [pltpu.VMEM((B, Wc), F32), pltpu.VMEM((nb, (T + 3) * B, LRU_BLOCK), F32)]
    if fwd:
        in_specs += [pl.BlockSpec((B, T, Wc), lambda j, s: (0, s, nj + j)),
                     pl.BlockSpec((T, B, Wc), lambda j, s: (s, 0, j))]
        args += [u, latb]
        scratch.append(pltpu.VMEM((nb, T * B, LRU_BLOCK), F32))
        out_specs = pl.BlockSpec((B, T, Wc), lambda j, s: (0, s, j))
        out_shape = jax.ShapeDtypeStruct((B, LT, W), BF16)
    else:
        out_specs = pl.BlockSpec((T, B, Wc), lambda j, s: (chunk(s), 0, j))
        out_shape = jax.ShapeDtypeStruct((LT, B, W), F32)
    return pl.pallas_call(
        functools.partial(_lru_kernel, fwd, T, n_ctx, n_tot),
        grid=(nj, n_tot),
        in_specs=in_specs,
        out_specs=out_specs,
        out_shape=out_shape,
        scratch_shapes=scratch,
        compiler_params=_cparams(("arbitrary", "arbitrary")),
        name="lru_fwd" if fwd else "lru_bwd",
    )(*args)


def _ret_kernel(fwd, C, lg_ref, blk_ref, q_ref, k_ref, v_ref, *rest):
    if fwd:
        gate_ref, ob_ref, o_ref, st_sc, qdec_sc, kdec_sc, mask_sc = rest
    else:
        o_ref, st_sc, qdec_sc, kdec_sc = rest
    s = pl.program_id(1)
    d = 0 if fwd else 1

    @pl.when(s == 0)
    def _():
        st_sc[...] = jnp.zeros_like(st_sc)

    @pl.when(jnp.logical_and(pl.program_id(0) == 0, s == 0))
    def _():
        row_v = lax.broadcasted_iota(jnp.int32, (C, RET_DV), 0).astype(F32)
        row_k = lax.broadcasted_iota(jnp.int32, (C, RET_DK), 0).astype(F32)
        if fwd:
            ri = lax.broadcasted_iota(jnp.int32, (C, C), 0)
            ci = lax.broadcasted_iota(jnp.int32, (C, C), 1)
            diff = (ri - ci).astype(F32)
        for hh in range(RET_HEADS):
            lg = lg_ref[d, hh]
            if fwd:
                qdec_sc[hh] = jnp.exp((row_v + 1.0) * lg)
                kdec_sc[hh] = jnp.exp((C - 1.0 - row_k) * lg)
                mask_sc[hh] = jnp.exp(jnp.where(diff >= 0, diff * lg_ref[0, hh],
                                                -diff * lg_ref[1, hh]))
            else:
                qdec_sc[hh] = jnp.exp((C - row_v) * lg)
                kdec_sc[hh] = jnp.exp(row_k * lg)

    for hh in range(RET_HEADS):
        qh = q_ref[0, :, hh * RET_DK:(hh + 1) * RET_DK]
        kh = k_ref[0, :, hh * RET_DK:(hh + 1) * RET_DK]
        vh = v_ref[0, :, hh * RET_DV:(hh + 1) * RET_DV]
        st = st_sc[hh]
        o = _dot(qh, st.astype(BF16)) * qdec_sc[hh]
        kd = (kh.astype(F32) * kdec_sc[hh]).astype(BF16)
        st_sc[hh] = st * blk_ref[d, hh] + _dot_tn(kd, vh)
        if fwd:
            sc = _dot_nt(qh, kh) * mask_sc[hh]
            o = o + _dot(sc.astype(BF16), vh)
            o = o + ob_ref[0, :, hh * RET_DV:(hh + 1) * RET_DV].astype(F32)
            o = o * lax.rsqrt(jnp.mean(o * o, axis=-1, keepdims=True) + NORM_EPS)
            gate = gate_ref[0, :, hh * RET_DV:(hh + 1) * RET_DV].astype(F32)
            o_ref[0, :, hh * RET_DV:(hh + 1) * RET_DV] = (_silu(gate) * o).astype(o_ref.dtype)
        else:
            o_ref[0, :, hh * RET_DV:(hh + 1) * RET_DV] = o.astype(o_ref.dtype)


def _ret_pass(fwd, u, logg, blk, n_ctx_rows, ob=None):
    B, LT, _ = u.shape
    C = RET_CHUNK
    nk = RET_HEADS * RET_DK
    nv = RET_HEADS * RET_DV
    n_tot = LT // C
    n_ctx = n_ctx_rows // C

    def chunk(s):
        if fwd:
            return s
        return jnp.where(s < n_ctx, n_ctx - 1 - s, n_tot + n_ctx - 1 - s)

    smem = pl.BlockSpec(memory_space=pltpu.SMEM)
    in_specs = [smem, smem,
                pl.BlockSpec((1, C, nk), lambda b, s: (b, chunk(s), 0)),
                pl.BlockSpec((1, C, nk), lambda b, s: (b, chunk(s), 1)),
                pl.BlockSpec((1, C, nv), lambda b, s: (b, chunk(s), 1))]
    args = [logg, blk, u, u, u]
    if fwd:
        in_specs += [pl.BlockSpec((1, C, nv), lambda b, s: (b, s, 2)),
                     pl.BlockSpec((1, C, nv), lambda b, s: (b, s, 0))]
        args += [u, ob]
    return pl.pallas_call(
        functools.partial(_ret_kernel, fwd, C),
        grid=(B, n_tot),
        in_specs=in_specs,
        out_specs=pl.BlockSpec((1, C, nv), lambda b, s: (b, chunk(s), 0)),
        out_shape=jax.ShapeDtypeStruct((B, LT, nv), BF16),
        scratch_shapes=[pltpu.VMEM((RET_HEADS, RET_DK, RET_DV), F32),
                        pltpu.VMEM((RET_HEADS, C, RET_DV), F32),
                        pltpu.VMEM((RET_HEADS, C, RET_DK), F32)]
        + ([pltpu.VMEM((RET_HEADS, C, C), F32)] if fwd else []),
        compiler_params=_cparams(("arbitrary", "arbitrary")),
        name="ret_fwd" if fwd else "ret_bwd",
    )(*args)


def _hy_filter_kernel(z_ref, tt_ref, fw1_ref, fb1_ref, fw2_ref, fb2_ref, fq_ref, w3_ref,
                      delta_ref, o_ref):
    fq = fq_ref[...]
    h1 = jnp.sin(fq * (_dot(z_ref[...], fw1_ref[...], HIGHEST) + fb1_ref[...]))
    h2 = jnp.sin(fq * (_dot(h1, fw2_ref[...], HIGHEST) + fb2_ref[...]))
    kt = _dot_nt(w3_ref[0], h2, HIGHEST)
    for r in range(tt_ref.shape[0]):
        t = tt_ref[r, 0:1, :]
        msk = tt_ref[r, 1:2, :]
        o_ref[r] = kt[:, r * LANES:(r + 1) * LANES] * jnp.exp(-delta_ref[...] * t) * msk


def _hy_circular(z, R):
    Lv = z.shape[0]
    N = R * LANES
    gap = N - 2 * Lv + 1
    zc = jnp.concatenate([z, jnp.zeros((gap, z.shape[1]), z.dtype), jnp.flip(z[1:], axis=0)], axis=0)
    valid = np.concatenate([np.ones(Lv), np.zeros(gap), np.ones(Lv - 1)]).astype(np.float32)
    return zc, valid


def _hy_filter_taps(Lv, R, fw1, fb1, fw2, fb2, fw3, freq):
    W = fw3.shape[1] // 4
    bands = (HY_EMB - 1) // 2
    t = jnp.linspace(0.0, 1.0, Lv, dtype=F32)[:, None]
    w = 2.0 * math.pi * jnp.arange(Lv, dtype=F32)[:, None] / Lv
    fr = jnp.linspace(1e-4, bands - 1, bands, dtype=F32)[None, :]
    z = jnp.concatenate([t, jnp.cos(fr * w), -jnp.sin(fr * w)], axis=-1)
    zc, valid = _hy_circular(z, R)
    zp = jnp.pad(zc, ((0, 0), (0, LANES - HY_EMB)))
    tt = jnp.stack([zc[:, 0].reshape(R, LANES), jnp.asarray(valid).reshape(R, LANES)], axis=1)
    fw1p = jnp.pad(fw1.astype(F32), ((0, LANES - HY_EMB), (0, 0)))
    w3 = fw3.astype(F32).reshape(HY_HIDDEN, 2, 2, W)
    w3d = jnp.stack([w3[:, :, 0, :].reshape(HY_HIDDEN, 2 * W).T,
                     w3[:, :, 1, :].reshape(HY_HIDDEN, 2 * W).T], axis=0)
    deltas = jnp.abs(jnp.linspace(math.log(HY_TARGET) / HY_SLOW_DECAY,
                                  math.log(HY_TARGET) / HY_FAST_DECAY, W, dtype=F32))
    delta2 = jnp.concatenate([deltas, deltas]).reshape(2 * W, 1)
    F2 = 2 * W
    rb = HY_TAP_ROWS
    assert (R // 2) % rb == 0
    half = R // 2 // rb
    kt = pl.pallas_call(
        _hy_filter_kernel,
        grid=(R // rb,),
        in_specs=[pl.BlockSpec((rb * LANES, LANES), lambda r: (r, 0)),
                  pl.BlockSpec((rb, 2, LANES), lambda r: (r, 0, 0)),
                  pl.BlockSpec((LANES, HY_HIDDEN), lambda r: (0, 0)),
                  pl.BlockSpec((1, HY_HIDDEN), lambda r: (0, 0)),
                  pl.BlockSpec((HY_HIDDEN, HY_HIDDEN), lambda r: (0, 0)),
                  pl.BlockSpec((1, HY_HIDDEN), lambda r: (0, 0)),
                  pl.BlockSpec((1, HY_HIDDEN), lambda r: (0, 0)),
                  pl.BlockSpec((1, F2, HY_HIDDEN), lambda r: (jnp.where(r < half, 0, 1), 0, 0)),
                  pl.BlockSpec((F2, 1), lambda r: (0, 0))],
        out_specs=pl.BlockSpec((rb, F2, LANES), lambda r: (r, 0, 0)),
        out_shape=jax.ShapeDtypeStruct((R, F2, LANES), F32),
        compiler_params=_cparams(("arbitrary",)),
        name="hyena_filter_taps",
    )(zp, tt, fw1p, fb1.astype(F32).reshape(1, -1), fw2.astype(F32), fb2.astype(F32).reshape(1, -1),
      freq.astype(F32).reshape(1, -1), w3d, delta2)
    return jnp.transpose(kt, (1, 0, 2))


def _dft_constants(R0, R):
    N = R * LANES
    k1 = np.arange(R)[:, None]
    n1 = np.arange(R0)[None, :]
    f1 = np.exp(-2j * np.pi * k1 * n1 / R)
    ma = np.block([[f1.real, -f1.imag], [f1.imag, f1.real]])
    g1 = np.exp(2j * np.pi * n1.T * k1.T / R) / N
    mai = np.block([[g1.real, -g1.imag], [g1.imag, g1.real]])
    n2 = np.arange(LANES)
    tw = np.exp(-2j * np.pi * np.arange(R)[:, None] * n2[None, :] / N)
    f2 = np.exp(-2j * np.pi * n2[:, None] * n2[None, :] / LANES)
    mc = np.block([[f2.real, f2.imag], [-f2.imag, f2.real]])
    mci = np.block([[f2.real, -f2.imag], [f2.imag, f2.real]])
    f1_full = np.exp(-2j * np.pi * k1 * np.arange(R)[None, :] / R)
    as_bf = lambda m: jnp.asarray(m, F32).astype(BF16)
    return dict(ma=as_bf(ma), mai=as_bf(mai), mc=as_bf(mc), mci=as_bf(mci),
                tr=jnp.asarray(tw.real, F32), ti=jnp.asarray(tw.imag, F32),
                ma_real=as_bf(np.concatenate([f1_full.real, f1_full.imag], axis=0)))


def _hy_fft_kernel(R, cb, k_ref, ma_ref, tr_ref, ti_ref, mc_ref, or_ref, oi_ref):
    cols = [k_ref[c] for c in range(cb)]
    inv = []
    for col in cols:
        sabs = jnp.sum(jnp.sum(jnp.abs(col), axis=1, keepdims=True), axis=0, keepdims=True)
        inv.append(1.0 / sabs)
    rhs = jnp.concatenate(cols, axis=1)
    hi = rhs.astype(BF16)
    lo = (rhs - hi.astype(F32)).astype(BF16)
    a = _dot(ma_ref[...], hi) + _dot(ma_ref[...], lo)
    tr = tr_ref[...]
    ti = ti_ref[...]
    rows = []
    for c in range(cb):
        ar = a[:R, c * LANES:(c + 1) * LANES]
        ai = a[R:, c * LANES:(c + 1) * LANES]
        rows.append(jnp.concatenate([ar * tr - ai * ti, ar * ti + ai * tr], axis=1))
    lhs = jnp.concatenate(rows, axis=0)
    hi = lhs.astype(BF16)
    lo = (lhs - hi.astype(F32)).astype(BF16)
    x = _dot(hi, mc_ref[...]) + _dot(lo, mc_ref[...])
    for c in range(cb):
        or_ref[c] = (x[c * R:(c + 1) * R, :LANES] * inv[c]).astype(or_ref.dtype)
        oi_ref[c] = (x[c * R:(c + 1) * R, LANES:] * inv[c]).astype(oi_ref.dtype)


def _hy_filter_spectrum(kt, R, consts):
    F2 = kt.shape[0]
    cb = HY_CB_SHORT
    spec = pl.BlockSpec((cb, R, LANES), lambda j: (j, 0, 0))
    full = lambda a: pl.BlockSpec(a.shape, lambda j: (0,) * a.ndim)
    return pl.pallas_call(
        functools.partial(_hy_fft_kernel, R, cb),
        grid=(F2 // cb,),
        in_specs=[spec, full(consts["ma_real"]), full(consts["tr"]), full(consts["ti"]),
                  full(consts["mc"])],
        out_specs=[spec, spec],
        out_shape=[jax.ShapeDtypeStruct((F2, R, LANES),
                                        BF16 if R % (2 * SUBLANES) == 0 else F32)] * 2,
        compiler_params=_cparams(("arbitrary",)),
        name="hyena_filter_fft",
    )(kt, consts["ma_real"], consts["tr"], consts["ti"], consts["mc"])


def _hy_conv_kernel(R0, R, Lv, cb, W, cw_ref, cbias_ref, skip_ref, v_ref, x1_ref, x2_ref, g_ref,
                    k0r_ref, k0i_ref, k1r_ref, k1i_ref, ma_ref, mai_ref, mc_ref, mci_ref,
                    tr_ref, ti_ref, o_ref):
    j = pl.program_id(0)
    row = lax.broadcasted_iota(jnp.int32, (R0, LANES), 0)
    lane = lax.broadcasted_iota(jnp.int32, (R0, LANES), 1)
    pos = row * LANES + lane
    cdt = BF16 if R % (2 * SUBLANES) == 0 else F32
    tr = tr_ref[...].astype(cdt)
    ti = ti_ref[...].astype(cdt)

    def cmul(xr, xi, yr, yi):
        return xr * yr - xi * yi, xr * yi + xi * yr

    def short_conv(ref, part, c, grp):
        x = ref[0, part, c].astype(F32)
        ch = grp * W + j * cb + c
        r = pltpu.roll(x, 1, 1)
        prev = jnp.where(lane == 0, pltpu.roll(r, 1, 0), r)
        prev = jnp.where(pos == 0, 0.0, prev)
        r = pltpu.roll(x, LANES - 1, 1)
        nxt = jnp.where(lane == LANES - 1, pltpu.roll(r, R0 - 1, 0), r)
        nxt = jnp.where(pos == R0 * LANES - 1, 0.0, nxt)
        y = cw_ref[0, ch] * prev + cw_ref[1, ch] * x + cw_ref[2, ch] * nxt + cbias_ref[ch]
        if Lv < R0 * LANES:
            y = jnp.where(pos < Lv, y, 0.0)
        return y

    def long_conv(re, im, kr_ref, ki_ref):
        rhs = jnp.concatenate([jnp.concatenate([re[c], im[c]], axis=0) for c in range(cb)], axis=1)
        a = _dot(ma_ref[...], rhs.astype(BF16))
        rows = []
        for c in range(cb):
            ar = a[:R, c * LANES:(c + 1) * LANES].astype(cdt)
            ai = a[R:, c * LANES:(c + 1) * LANES].astype(cdt)
            rows.append(jnp.concatenate(cmul(ar, ai, tr, ti), axis=1))
        x = _dot(jnp.concatenate(rows, axis=0).astype(BF16), mc_ref[...])
        rows = []
        for c in range(cb):
            xr = x[c * R:(c + 1) * R, :LANES].astype(cdt)
            xi = x[c * R:(c + 1) * R, LANES:].astype(cdt)
            rows.append(jnp.concatenate(
                cmul(xr, xi, kr_ref[c].astype(cdt), ki_ref[c].astype(cdt)), axis=1))
        b = _dot(jnp.concatenate(rows, axis=0).astype(BF16), mci_ref[...])
        cols = []
        for c in range(cb):
            br = b[c * R:(c + 1) * R, :LANES].astype(cdt)
            bi = b[c * R:(c + 1) * R, LANES:].astype(cdt)
            cols.append(jnp.concatenate(cmul(br, bi, tr, -ti), axis=0))
        y = _dot(mai_ref[...], jnp.concatenate(cols, axis=1).astype(BF16))
        out_re = [y[:R0, c * LANES:(c + 1) * LANES] for c in range(cb)]
        out_im = [y[R0:, c * LANES:(c + 1) * LANES] for c in range(cb)]
        return out_re, out_im

    v = [[short_conv(v_ref, p, c, 0) for c in range(cb)] for p in range(2)]
    x1 = [[short_conv(x1_ref, p, c, 1) for c in range(cb)] for p in range(2)]
    cr, ci = long_conv(v[0], v[1], k0r_ref, k0i_ref)
    conv = (cr, ci)
    y = [[x1[p][c] * (conv[p][c] + v[p][c] * skip_ref[0, j * cb + c]) for c in range(cb)]
         for p in range(2)]
    x2 = [[short_conv(x2_ref, p, c, 2) for c in range(cb)] for p in range(2)]
    cr, ci = long_conv(y[0], y[1], k1r_ref, k1i_ref)
    conv = (cr, ci)
    for p in range(2):
        for c in range(cb):
            y2 = x2[p][c] * (conv[p][c] + y[p][c] * skip_ref[1, j * cb + c])
            gate = g_ref[0, p, c].astype(F32)
            o_ref[0, p, c] = (y2 * _silu(gate)).astype(o_ref.dtype)


def _hy_conv(ut, kfr, kfi, conv_w, conv_b, skip, Lv, R0, R, consts):
    G, _, W4, _, _ = ut.shape
    W = W4 // 4
    cb = HY_CB if R0 * LANES >= HY_SHORT_SEQ else HY_CB_SHORT
    nj = W // cb
    smem = pl.BlockSpec(memory_space=pltpu.SMEM)
    full = lambda a: pl.BlockSpec(a.shape, lambda j, g: (0,) * a.ndim)

    def data(grp):
        return pl.BlockSpec((1, 2, cb, R0, LANES), lambda j, g: (g, 0, grp * nj + j, 0, 0))

    def filt(order):
        return pl.BlockSpec((cb, R, LANES), lambda j, g: (order * nj + j, 0, 0))

    return pl.pallas_call(
        functools.partial(_hy_conv_kernel, R0, R, Lv, cb, W),
        grid=(nj, G),
        in_specs=[smem, smem, smem, data(0), data(1), data(2), data(3),
                  filt(0), filt(0), filt(1), filt(1),
                  full(consts["ma"]), full(consts["mai"]), full(consts["mc"]), full(consts["mci"]),
                  full(consts["tr"]), full(consts["ti"])],
        out_specs=pl.BlockSpec((1, 2, cb, R0, LANES), lambda j, g: (g, 0, j, 0, 0)),
        out_shape=jax.ShapeDtypeStruct((G, 2, W, R0, LANES), BF16),
        compiler_params=_cparams(("arbitrary", "arbitrary")),
        name="hyena_conv",
    )(conv_w, conv_b, skip, ut, ut, ut, ut, kfr, kfi, kfr, kfi,
      consts["ma"], consts["mai"], consts["mc"], consts["mci"], consts["tr"], consts["ti"])


def _hy_segment(ut, rows, conv_w, conv_b, fw1, fb1, fw2, fb2, fw3, freq, skip):
    B, W4, rows_in, _ = ut.shape
    W = W4 // 4
    Lv = rows * LANES
    R0 = -(-rows // SUBLANES) * SUBLANES
    R = max(R0, -(-(2 * rows) // SUBLANES) * SUBLANES)
    consts = _dft_constants(R0, R)
    kt = _hy_filter_taps(Lv, R, fw1, fb1, fw2, fb2, fw3, freq)
    kfr, kfi = _hy_filter_spectrum(kt, R, consts)
    if R0 > rows_in:
        ut = jnp.pad(ut, ((0, 0), (0, 0), (0, R0 - rows_in), (0, 0)))
    else:
        assert R0 == rows
    ut = ut.reshape(B // 2, 2, W4, ut.shape[2], LANES)
    z = _hy_conv(ut, kfr, kfi, conv_w.astype(F32), conv_b.astype(F32), skip.astype(F32),
                 Lv, R0, R, consts)
    z = z.reshape(B, W, R0, LANES)[:, :, :rows]
    return jnp.transpose(z, (0, 2, 3, 1)).reshape(B, Lv, W)


def _attn_kernel(n_chunks, ck, q0_ref, q1_ref, q2_ref, k_ref, v_ref, g_ref, o_ref,
                 sa_sc, sb_sc, ma_sc):
    R = ATT_HEADS // ATT_KV
    tq = q1_ref.shape[1]

    def stack_heads(q_ref):
        return jnp.concatenate([q_ref[0, :, r * ATT_D:(r + 1) * ATT_D] for r in range(R)], axis=0)

    ncb = ck // LANES

    def scores(q, s_sc, c):
        s = _dot_nt(q, k_ref[0, c * ck:(c + 1) * ck, :])
        s_sc[:, c * ck:(c + 1) * ck] = s
        mp = s[:, :LANES]
        for j in range(1, ncb):
            mp = jnp.maximum(mp, s[:, j * LANES:(j + 1) * LANES])
        return mp

    def weighted(s_sc, m_rep, c):
        ps = [jnp.exp2(s_sc[:, c * ck + j * LANES:c * ck + (j + 1) * LANES] - m_rep)
              for j in range(ncb)]
        lp = ps[0]
        for j in range(1, ncb):
            lp = lp + ps[j]
        p = jnp.concatenate([pj.astype(BF16) for pj in ps], axis=1)
        return lp, _dot(p, v_ref[0, c * ck:(c + 1) * ck, :])

    def lane_max_rep(mp):
        return jnp.broadcast_to(jnp.max(mp, axis=-1, keepdims=True), mp.shape)

    def finish(acc, lp, half):
        o = acc / jnp.sum(lp, axis=-1, keepdims=True)
        for r in range(R):
            rows = slice(half * tq, (half + 1) * tq)
            gate = g_ref[0, rows, r * ATT_D:(r + 1) * ATT_D].astype(F32)
            o_ref[0, rows, r * ATT_D:(r + 1) * ATT_D] = (
                _silu(gate) * o[r * tq:(r + 1) * tq]).astype(o_ref.dtype)

    def phase(s_cur, m_cur, q_next, s_next):
        m_next = l = acc = None
        for c in range(n_chunks):
            mc = scores(q_next, s_next, c)
            lc, pv = weighted(s_cur, m_cur, c)
            m_next = mc if m_next is None else jnp.maximum(m_next, mc)
            l = lc if l is None else l + lc
            acc = pv if acc is None else acc + pv
        return acc, l, lane_max_rep(m_next)

    @pl.when(pl.program_id(2) == 0)
    def _():
        q0 = stack_heads(q0_ref)
        m = None
        for c in range(n_chunks):
            mc = scores(q0, sa_sc, c)
            m = mc if m is None else jnp.maximum(m, mc)
        ma_sc[...] = lane_max_rep(m)

    acc, l, mb = phase(sa_sc, ma_sc[...], stack_heads(q1_ref), sb_sc)
    finish(acc, l, 0)
    acc, l, ma = phase(sb_sc, mb, stack_heads(q2_ref), sa_sc)
    finish(acc, l, 1)
    ma_sc[...] = ma


def _attention(u, n_ctx_rows):
    B, LT, _ = u.shape
    L = LT - n_ctx_rows
    R = ATT_HEADS // ATT_KV
    tq = ATT_TQ
    ck = max(t for t in range(LANES, ATT_CHUNK_MAX + 1, LANES) if LT % t == 0)
    qw = R * ATT_D
    n_tiles = L // tq
    assert n_tiles % 2 == 0 and n_ctx_rows % (2 * tq) == 0
    q_off = n_ctx_rows // tq
    k_col = ATT_HEADS
    v_col = ATT_HEADS + ATT_KV
    g_col = (ATT_HEADS + 2 * ATT_KV) * ATT_D // qw

    def q_spec(tile):
        return pl.BlockSpec((1, tq, qw), lambda b, g, i: (b, q_off + tile(i), g))

    return pl.pallas_call(
        functools.partial(_attn_kernel, LT // ck, ck),
        grid=(B, ATT_KV, n_tiles // 2),
        in_specs=[q_spec(lambda i: 0),
                  q_spec(lambda i: 2 * i + 1),
                  q_spec(lambda i: jnp.minimum(2 * i + 2, n_tiles - 1)),
                  pl.BlockSpec((1, LT, ATT_D), lambda b, g, i: (b, 0, k_col + g),
                               pipeline_mode=pl.Buffered(1)),
                  pl.BlockSpec((1, LT, ATT_D), lambda b, g, i: (b, 0, v_col + g),
                               pipeline_mode=pl.Buffered(1)),
                  pl.BlockSpec((1, 2 * tq, qw), lambda b, g, i: (b, i + q_off // 2, g_col + g))],
        out_specs=pl.BlockSpec((1, 2 * tq, qw), lambda b, g, i: (b, i, g)),
        out_shape=jax.ShapeDtypeStruct((B, L, ATT_HEADS * ATT_D), BF16),
        scratch_shapes=[pltpu.VMEM((R * tq, LT), F32), pltpu.VMEM((R * tq, LT), F32),
                        pltpu.VMEM((R * tq, LANES), F32)],
        compiler_params=_cparams(("arbitrary", "arbitrary", "arbitrary")),
        name="attention",
    )(u, u, u, u, u, u)


def _rope_angles(pos, dim):
    half = dim // 2
    inv = ROPE_THETA ** (-jnp.arange(half, dtype=F32) / half)
    return pos[:, None] * inv[None, :]


def _with_ctx_identity(cos_like, sin_like, n_ctx_rows):
    ones = jnp.ones((n_ctx_rows, cos_like.shape[1]), F32)
    return (jnp.concatenate([ones, cos_like], axis=0),
            jnp.concatenate([jnp.zeros_like(ones), sin_like], axis=0))


def _layer_lru(xs, Lc, ms, norm_g, w_in, conv_w, conv_b, w_r, b_r, w_i, b_i, lam, w_out):
    n_ctx_blk = Lc // TOK_TILE
    u = _norm_proj(xs, ms, norm_g, w_in.astype(BF16), n_ctx_blk)
    lat = None
    for d in (1, 0):
        wg = (0.5 * jnp.concatenate([w_r[d], w_i[d]], axis=-1)).astype(BF16)
        bg = 0.5 * jnp.stack([b_r[d], b_i[d]], axis=0).astype(F32)
        lat = _lru_pass(d == 0, u, conv_w.astype(F32), conv_b.astype(F32), wg, bg,
                        lam[d].astype(F32), Lc, latb=lat)
    return _out_proj_residual(lat, w_out.astype(BF16), xs, ms, n_ctx_blk)


def _layer_ret(xs, Lc, ms, norm_g, w_in, decay_logit, w_out):
    n_ctx_blk = Lc // TOK_TILE
    L = xs.shape[1] - Lc
    ang = _rope_angles(jnp.arange(L, dtype=F32), RET_DK)
    cos, sin = _with_ctx_identity(jnp.cos(ang), jnp.sin(ang), Lc)
    u = _norm_proj(xs, ms, norm_g, w_in.astype(BF16), n_ctx_blk, "ret", (cos, sin))
    dl = decay_logit.astype(F32)
    logg = -(jnp.maximum(-dl, 0.0) + jnp.log1p(jnp.exp(-jnp.abs(dl))))
    blk = jnp.exp(RET_CHUNK * logg)
    ob = _ret_pass(False, u, logg, blk, Lc)
    z = _ret_pass(True, u, logg, blk, Lc, ob=ob)
    return _out_proj_residual(z, w_out.astype(BF16), xs, ms, n_ctx_blk)


def _layer_hyena(xs, Lc, ms, norm_g, w_in, conv_w, conv_b, fw1, fb1, fw2, fb2, fw3, freq, skip,
                 w_out):
    n_ctx_blk = Lc // TOK_TILE
    L = xs.shape[1] - Lc
    u = _norm_proj(xs, ms, norm_g, w_in.astype(BF16), n_ctx_blk, ctx_last=True)
    hy_args = (conv_w, conv_b, fw1, fb1, fw2, fb2, fw3, freq, skip)
    assert Lc % LANES == 0 and L % LANES == 0
    u4 = u.reshape(u.shape[0], (L + Lc) // LANES, LANES, u.shape[2])
    ut = jnp.transpose(u4, (0, 3, 1, 2))
    z = jnp.concatenate([_hy_segment(ut[:, :, L // LANES:], Lc // LANES, *hy_args),
                         _hy_segment(ut, L // LANES, *hy_args)], axis=1)
    return _out_proj_residual(z, w_out.astype(BF16), xs, ms, n_ctx_blk)


def _layer_attn(xs, Lc, ms, norm_g, w_in, q_norm_g, k_norm_g, w_out, final_g=None):
    n_ctx_blk = Lc // TOK_TILE
    L = xs.shape[1] - Lc
    rows = L // GRID_W
    row = jnp.repeat(jnp.arange(rows, dtype=F32), GRID_W)
    col = jnp.tile(jnp.arange(GRID_W, dtype=F32), rows)
    ang = jnp.concatenate([_rope_angles(row, ATT_D // 2), _rope_angles(col, ATT_D // 2)], axis=-1)
    cos, sin = _with_ctx_identity(jnp.cos(ang), jnp.sin(ang), Lc)
    c2 = jnp.concatenate([cos, cos], axis=-1)
    s2 = jnp.concatenate([-sin, sin], axis=-1)
    u = _norm_proj(xs, ms, norm_g, w_in.astype(BF16), n_ctx_blk, "att",
                   (c2, s2, q_norm_g.astype(F32).reshape(1, ATT_D),
                    k_norm_g.astype(F32).reshape(1, ATT_D)))
    z = _attention(u, Lc)
    return _out_proj_residual(z, w_out.astype(BF16), xs, ms, n_ctx_blk, x_off_blk=n_ctx_blk,
                              final_g=final_g)


def kernel(x, c, ctx, c_ctx, lru_mod_w, lru_mod_b, lru_norm_g, lru_w_in, lru_conv_w, lru_conv_b, lru_w_r, lru_b_r, lru_w_i, lru_b_i, lru_lambda, lru_w_out, ret_mod_w, ret_mod_b, ret_norm_g, ret_w_in, ret_decay_logit, ret_w_out, hy_mod_w, hy_mod_b, hy_norm_g, hy_w_in, hy_conv_w, hy_conv_b, hy_fw1, hy_fb1, hy_fw2, hy_fb2, hy_fw3, hy_freq, hy_skip, hy_w_out, att_mod_w, att_mod_b, att_norm_g, att_w_in, att_q_norm_g, att_k_norm_g, att_w_out, final_norm_g):
    B, L, D = x.shape
    Lc = ctx.shape[1]
    LT = Lc + L
    assert B == SUBLANES and B % 2 == 0
    assert Lc % TOK_TILE == 0 and L % TOK_TILE == 0 and Lc % RET_CHUNK == 0 and Lc % LRU_T == 0
    n_ctx_blk = Lc // TOK_TILE

    cvec = jnp.zeros((16, D), F32).at[:B].set(c).at[B].set(c_ctx)

    def modsel(mod_w, mod_b):
        m = _modulation(cvec, mod_w, mod_b).reshape(16, 3, D)
        return jnp.stack([jnp.broadcast_to(m[B][None], (B, 3, D)), m[:B]], axis=0)

    xs = _layer_lru((ctx, x), Lc, modsel(lru_mod_w, lru_mod_b), lru_norm_g, lru_w_in, lru_conv_w,
                    lru_conv_b, lru_w_r, lru_b_r, lru_w_i, lru_b_i, lru_lambda, lru_w_out)
    xs = _layer_ret(xs, Lc, modsel(ret_mod_w, ret_mod_b), ret_norm_g, ret_w_in, ret_decay_logit,
                    ret_w_out)
    xs = _layer_hyena(xs, Lc, modsel(hy_mod_w, hy_mod_b), hy_norm_g, hy_w_in, hy_conv_w, hy_conv_b,
                      hy_fw1, hy_fb1, hy_fw2, hy_fb2, hy_fw3, hy_freq, hy_skip, hy_w_out)
    return _layer_attn(xs, Lc, modsel(att_mod_w, att_mod_b), att_norm_g, att_w_in, att_q_norm_g,
                       att_k_norm_g, att_w_out, final_g=final_norm_g)
```
